```python
import jax, jax.numpy as jnp
from jax import lax
import numpy as np

D_MODEL = 4096
BATCH = 4
SEQ = 4096
DEPTH = 1

GRID_W = 64
CTX_LEN = 256
NA_HEADS = 16
HEAD_DIM = 128
D_NA = NA_HEADS * HEAD_DIM
NA_KH_MAX = 8
NA_KW = 16
D_SGU = D_MODEL - D_NA
SGU_GROUPS = 4
SGU_GROUP_DIM = D_SGU // SGU_GROUPS
SGU_CHUNK = 128
D_IN = 3 * D_NA + 2 * D_SGU
D_FF = 4 * D_MODEL
N_MOD = 6
NORM_EPS = 1e-6

kernel_name = "hybrid_na_sgu_dit_layer"


def rmsnorm(x, w):
    xf = x.astype(jnp.float32)
    y = xf * lax.rsqrt(jnp.mean(xf * xf, axis=-1, keepdims=True) + NORM_EPS)
    return (y * w.astype(jnp.float32)).astype(x.dtype)


def adaln(cvec, w_ada, b_ada):
    m = (jax.nn.silu(cvec) @ w_ada + b_ada)[..., None, :]
    return jnp.split(m, N_MOD, axis=-1)


def modulate(h, shift, scale):
    return h * (1 + scale) + shift


def split_proj(proj):
    return jnp.split(proj, [D_NA, 2 * D_NA, 3 * D_NA, 3 * D_NA + D_SGU], axis=-1)


def to_heads(t):
    b, l, _ = t.shape
    return t.reshape(b, l, NA_HEADS, HEAD_DIM)


def neighbourhood_attention(q, k, v, k_ctx, v_ctx, rpb):
    b, s, h, dh = q.shape
    rows = s // GRID_W
    kh = min(NA_KH_MAX, rows)
    scale = dh ** -0.5
    qg = (q * scale).reshape(b, rows, GRID_W, h, dh)
    kg = k.reshape(b, rows, GRID_W, h, dh)
    vg = v.reshape(b, rows, GRID_W, h, dh)
    cols = jnp.arange(GRID_W)
    col_start = jnp.clip(cols - NA_KW // 2, 0, GRID_W - NA_KW)
    key_cols = col_start[:, None] + jnp.arange(NA_KW)[None, :]
    col_off = key_cols - cols[:, None] + (NA_KW - 1)
    n_win = kh * NA_KW

    def row_block(r):
        r0 = jnp.clip(r - kh // 2, 0, rows - kh)
        q_r = lax.dynamic_index_in_dim(qg, r, axis=1, keepdims=False)
        k_rows = lax.dynamic_slice_in_dim(kg, r0, kh, axis=1)
        v_rows = lax.dynamic_slice_in_dim(vg, r0, kh, axis=1)
        k_win = k_rows[:, :, key_cols]
        v_win = v_rows[:, :, key_cols]
        s_win = jnp.einsum('bqhd,biqjhd->bhqij', q_r, k_win)
        row_off = r0 + jnp.arange(kh) - r + (NA_KH_MAX - 1)
        bias = rpb[:, row_off][:, :, col_off]
        s_win = s_win + jnp.transpose(bias, (0, 2, 1, 3))[None]
        s_ctx = jnp.einsum('bqhd,bchd->bhqc', q_r, k_ctx)
        scores = jnp.concatenate([s_win.reshape(b, h, GRID_W, n_win), s_ctx], axis=-1)
        p = jax.nn.softmax(scores.astype(jnp.float32), axis=-1).astype(v.dtype)
        p_win = p[..., :n_win].reshape(b, h, GRID_W, kh, NA_KW)
        p_ctx = p[..., n_win:]
        return (jnp.einsum('bhqij,biqjhd->bqhd', p_win, v_win)
                + jnp.einsum('bhqc,bchd->bqhd', p_ctx, v_ctx))

    out = lax.map(row_block, jnp.arange(rows))
    return jnp.transpose(out, (1, 0, 2, 3, 4)).reshape(b, s, h * dh)


def context_attention(q, k, v):
    b, l, h, dh = q.shape
    s = jnp.einsum('bqhd,bkhd->bhqk', q * dh ** -0.5, k)
    p = jax.nn.softmax(s.astype(jnp.float32), axis=-1).astype(v.dtype)
    return jnp.einsum('bhqk,bkhd->bqhd', p, v).reshape(b, l, h * dh)


def spatial_gating(u, g, w_s, b_s, norm_w):
    b, l, _ = u.shape
    gn = rmsnorm(g, norm_w).reshape(b, l // SGU_CHUNK, SGU_CHUNK, SGU_GROUPS, SGU_GROUP_DIM)
    mixed = jnp.einsum('gpq,bnqgc->bnpgc', w_s, gn) + jnp.transpose(b_s)[None, None, :, :, None]
    return u * mixed.reshape(b, l, D_SGU)


def merge_groups(o_na, o_sgu, gn_na, gn_sgu, w_out):
    return jnp.concatenate([rmsnorm(o_na, gn_na), rmsnorm(o_sgu, gn_sgu)], axis=-1) @ w_out


def squared_relu_mlp(h, w1, w2):
    return jnp.square(jax.nn.relu(h @ w1)) @ w2


def setup_inputs(seed: int = 0) -> dict:
    key = jax.random.key(seed)
    ks = jax.random.split(key, 20)
    f32 = jnp.float32
    nrm = lambda k, shape, s: jax.random.normal(k, shape, f32) * s
    gain = lambda k, shape: 1.0 + 0.01 * jax.random.normal(k, shape, f32)
    return {
        "x": nrm(ks[0], (BATCH, SEQ, D_MODEL), 1.0),
        "c": nrm(ks[1], (BATCH, D_MODEL), 1.0),
        "ctx": nrm(ks[2], (BATCH, CTX_LEN, D_MODEL), 1.0),
        "c_ctx": nrm(ks[3], (D_MODEL,), 1.0),
        "w_ada": nrm(ks[4], (DEPTH, D_MODEL, N_MOD * D_MODEL), D_MODEL ** -0.5),
        "b_ada": nrm(ks[5], (DEPTH, N_MOD * D_MODEL), 0.01),
        "norm1_w": gain(ks[6], (DEPTH, D_MODEL)),
        "w_in": nrm(ks[7], (DEPTH, D_MODEL, D_IN), D_MODEL ** -0.5),
        "rpb": nrm(ks[8], (DEPTH, NA_HEADS, 2 * NA_KH_MAX - 1, 2 * NA_KW - 1), 0.1),
        "sgu_norm_w": gain(ks[9], (DEPTH, D_SGU)),
        "sgu_w": nrm(ks[10], (DEPTH, SGU_GROUPS, SGU_CHUNK, SGU_CHUNK), SGU_CHUNK ** -0.5),
        "sgu_b": gain(ks[11], (DEPTH, SGU_GROUPS, SGU_CHUNK)),
        "grp_norm_na": gain(ks[12], (DEPTH, D_NA)),
        "grp_norm_sgu": gain(ks[13], (DEPTH, D_SGU)),
        "w_out": nrm(ks[14], (DEPTH, D_MODEL, D_MODEL), D_MODEL ** -0.5),
        "norm2_w": gain(ks[15], (DEPTH, D_MODEL)),
        "w_ff1": nrm(ks[16], (DEPTH, D_MODEL, D_FF), D_MODEL ** -0.5),
        "w_ff2": nrm(ks[17], (DEPTH, D_FF, D_MODEL), D_FF ** -0.5),
        "final_norm_w": gain(ks[18], (D_MODEL,)),
    }


def reference(x, c, ctx, c_ctx, w_ada, b_ada, norm1_w, w_in, rpb, sgu_norm_w, sgu_w, sgu_b,
              grp_norm_na, grp_norm_sgu, w_out, norm2_w, w_ff1, w_ff2, final_norm_w):
    for l in range(DEPTH):
        last = l == DEPTH - 1
        sh1, sc1, g1, sh2, sc2, g2 = adaln(c, w_ada[l], b_ada[l])
        csh1, csc1, cg1, csh2, csc2, cg2 = adaln(c_ctx, w_ada[l], b_ada[l])

        h = modulate(rmsnorm(x, norm1_w[l]), sh1, sc1)
        hc = modulate(rmsnorm(ctx, norm1_w[l]), csh1, csc1)
        q, k, v, u, gt = split_proj(h @ w_in[l])
        if last:
            kc, vc = jnp.split(hc @ w_in[l][:, D_NA:3 * D_NA], 2, axis=-1)
        else:
            qc, kc, vc, uc, gc = split_proj(hc @ w_in[l])
        kc_h, vc_h = to_heads(kc), to_heads(vc)
        o_na = neighbourhood_attention(to_heads(q), to_heads(k), to_heads(v), kc_h, vc_h, rpb[l])
        o_sgu = spatial_gating(jax.nn.gelu(u), jax.nn.gelu(gt), sgu_w[l], sgu_b[l], sgu_norm_w[l])
        x = x + g1 * merge_groups(o_na, o_sgu, grp_norm_na[l], grp_norm_sgu[l], w_out[l])

        h2 = modulate(rmsnorm(x, norm2_w[l]), sh2, sc2)
        x = x + g2 * squared_relu_mlp(h2, w_ff1[l], w_ff2[l])

        if not last:
            oc_na = context_attention(to_heads(qc), kc_h, vc_h)
            oc_sgu = spatial_gating(jax.nn.gelu(uc), jax.nn.gelu(gc), sgu_w[l], sgu_b[l], sgu_norm_w[l])
            ctx = ctx + cg1 * merge_groups(oc_na, oc_sgu, grp_norm_na[l], grp_norm_sgu[l], w_out[l])
            hc2 = modulate(rmsnorm(ctx, norm2_w[l]), csh2, csc2)
            ctx = ctx + cg2 * squared_relu_mlp(hc2, w_ff1[l], w_ff2[l])

    return rmsnorm(x, final_norm_w)
```

```python
import functools

import jax
import jax.numpy as jnp
from jax import lax
from jax.experimental import pallas as pl
from jax.experimental.pallas import tpu as pltpu

F32 = jnp.float32
BF16 = jnp.bfloat16

GRID_W = 64
NA_HEADS = 16
HEAD_DIM = 128
NA_KH_MAX = 8
NA_KW = 16
SGU_GROUPS = 4
SGU_CHUNK = 128
N_MOD = 6
NORM_EPS = 1e-6
MASK_VALUE = -1e30

ATT_QROWS = 8
ATT_KROWS = 16

V7X_VMEM_BYTES = 64 << 20


def _params(semantics, vmem_mib):
    assert (vmem_mib << 20) < V7X_VMEM_BYTES
    return pltpu.CompilerParams(dimension_semantics=semantics, vmem_limit_bytes=vmem_mib << 20)


def _tile(n, pref, align=128, also=0):
    for t in range(min(pref, n) // align * align, 0, -align):
        if n % t == 0 and also % t == 0:
            return t
    assert also == 0
    return n


def _adaln_kernel(c_ref, w_ref, b_ref, o_ref):
    c = c_ref[...]
    a = (c * jax.nn.sigmoid(c)).astype(BF16)
    o_ref[...] = jnp.dot(a, w_ref[...].astype(BF16), preferred_element_type=F32) + b_ref[...]


def _adaln(cvecs, w_ada, b_ada):
    r, d = cvecs.shape
    n = w_ada.shape[1]
    tn = _tile(n, 512)
    return pl.pallas_call(
        _adaln_kernel,
        grid=(n // tn,),
        in_specs=[
            pl.BlockSpec((r, d), lambda j: (0, 0)),
            pl.BlockSpec((d, tn), lambda j: (0, j)),
            pl.BlockSpec((1, tn), lambda j: (0, j)),
        ],
        out_specs=pl.BlockSpec((r, tn), lambda j: (0, j)),
        out_shape=jax.ShapeDtypeStruct((r, n), F32),
        compiler_params=_params(("arbitrary",), 40),
        name="adaln",
    )(cvecs, w_ada, b_ada.reshape(1, n))


def _norm_mod_matmul_kernel(x_ref, nw_ref, sc_ref, sh_ref, w_ref, cs_ref, o_ref, h_ref, *,
                            rows_chunk, act):
    tm = x_ref.shape[0]

    @pl.when(pl.program_id(2) == 0)
    def _():
        def body(c, carry):
            r = pl.multiple_of(c * rows_chunk, rows_chunk)
            x = x_ref[pl.ds(r, rows_chunk), :]
            ms = jnp.mean(x * x, axis=-1, keepdims=True)
            y = x * lax.rsqrt(ms + NORM_EPS) * nw_ref[...]
            h = y * (1.0 + sc_ref[...]) + sh_ref[...]
            h_ref[pl.ds(r, rows_chunk), :] = h.astype(BF16)
            return carry
        lax.fori_loop(0, tm // rows_chunk, body, 0)

    acc = jnp.dot(h_ref[...], w_ref[...], preferred_element_type=F32)
    if cs_ref is not None:
        acc = acc * cs_ref[...]
    if act == "relu2":
        acc = jnp.square(jnp.maximum(acc, 0.0))
    o_ref[...] = acc.astype(o_ref.dtype)


def _norm_mod_matmul(x, norm_w, scale, shift, w, *, col_scale=None, act=None, n_lo=0, n_hi=None,
                     tm=512, tn=1024, vmem_mib=48):
    b, s, k = x.shape
    n_hi = w.shape[1] if n_hi is None else n_hi
    n = n_hi - n_lo
    tm = _tile(s, tm, align=8)
    tn = _tile(n, tn, also=n_lo)
    j0 = n_lo // tn
    per_batch = scale.shape[0] == b and b > 1
    mod_map = (lambda bb, i, j: (bb, 0, 0)) if per_batch else (lambda bb, i, j: (0, 0, 0))
    in_specs = [
        pl.BlockSpec((None, tm, k), lambda bb, i, j: (bb, i, 0)),
        pl.BlockSpec((1, k), lambda bb, i, j: (0, 0)),
        pl.BlockSpec((None, 1, k), mod_map),
        pl.BlockSpec((None, 1, k), mod_map),
        pl.BlockSpec((k, tn), lambda bb, i, j: (0, j0 + j)),
    ]
    args = [x, norm_w.reshape(1, k), scale, shift, w]
    if col_scale is not None:
        in_specs.append(pl.BlockSpec((1, tn), lambda bb, i, j: (0, j)))
        args.append(col_scale.reshape(1, n))
        kern = _norm_mod_matmul_kernel
    else:
        def kern(x_ref, nw_ref, sc_ref, sh_ref, w_ref, o_ref, h_ref, **kw):
            _norm_mod_matmul_kernel(x_ref, nw_ref, sc_ref, sh_ref, w_ref, None, o_ref, h_ref, **kw)
    return pl.pallas_call(
        functools.partial(kern, rows_chunk=min(64, tm), act=act),
        grid=(b, s // tm, n // tn),
        in_specs=in_specs,
        out_specs=pl.BlockSpec((None, tm, tn), lambda bb, i, j: (bb, i, j)),
        out_shape=jax.ShapeDtypeStruct((b, s, n), BF16),
        scratch_shapes=[pltpu.VMEM((tm, k), BF16)],
        compiler_params=_params(("parallel", "parallel", "arbitrary"), vmem_mib),
        name="norm_mod_matmul" + ("_relu2" if act else ""),
    )(*args)


def _window_structure(rows, rblk):
    kh = min(NA_KH_MAX, rows)
    kstart = min(max(rblk * ATT_QROWS - kh // 2, 0), rows - ATT_KROWS)
    out = []
    for i in range(ATT_QROWS):
        qr = rblk * ATT_QROWS + i
        r0 = min(max(qr - kh // 2, 0), rows - kh)
        row = []
        for j in range(ATT_KROWS):
            kr = kstart + j
            row.append(kr - qr + NA_KH_MAX - 1 if r0 <= kr < r0 + kh else None)
        assert sum(e is not None for e in row) == kh
        out.append(tuple(row))
    return tuple(out)


def _block_types(rows):
    nblk = rows // ATT_QROWS
    reps = [0, min(1, nblk - 1), nblk - 1]
    structs = [_window_structure(rows, r) for r in reps]
    for r in range(nblk):
        t = 0 if r == 0 else (2 if r == nblk - 1 else 1)
        assert _window_structure(rows, r) == structs[t]
    return structs


def _attn_kernel(rpb_ref, q_ref, k_ref, v_ref, kc_ref, vc_ref, o_ref, bias_ref, *, rows):
    w = GRID_W
    n_ro = 2 * NA_KH_MAX - 1
    n_co = 2 * NA_KW - 1
    nblk = rows // ATT_QROWS
    kh = min(NA_KH_MAX, rows)
    h = pl.program_id(0)
    b = pl.program_id(1)
    r = pl.program_id(2)

    @pl.when((b == 0) & (r == 0))
    def _():
        qc = lax.broadcasted_iota(jnp.int32, (w, 2 * w), 0)
        lane = lax.broadcasted_iota(jnp.int32, (w, 2 * w), 1)
        kc = lane % w
        cs = jnp.clip(qc - NA_KW // 2, 0, w - NA_KW)
        col_ok = (kc >= cs) & (kc < cs + NA_KW)
        diff = kc - qc + (NA_KW - 1)
        hit = [(diff == d) & col_ok for d in range(n_co)]
        base = h * (n_ro * n_co)
        neg = jnp.full((w, 2 * w), MASK_VALUE, F32)
        tables = []
        for ro in range(n_ro):
            acc = neg
            for d in range(n_co):
                acc = jnp.where(hit[d], rpb_ref[base + ro * n_co + d], acc)
            tables.append(acc)
        left = lane < w
        for t, struct in enumerate(_block_types(rows)):
            for i in range(ATT_QROWS):
                for jp in range(ATT_KROWS // 2):
                    ro_l, ro_r = struct[i][2 * jp], struct[i][2 * jp + 1]
                    blk_l = neg if ro_l is None else tables[ro_l]
                    blk_r = neg if ro_r is None else tables[ro_r]
                    blk = blk_l if ro_l == ro_r else jnp.where(left, blk_l, blk_r)
                    bias_ref[t, i * w:(i + 1) * w, jp * 2 * w:(jp + 1) * 2 * w] = blk

    kstart = jnp.clip(r * ATT_QROWS - kh // 2, 0, rows - ATT_KROWS)
    start = pl.multiple_of(kstart * w, w)
    kw = k_ref[pl.ds(start, ATT_KROWS * w), :]
    vw = v_ref[pl.ds(start, ATT_KROWS * w), :]
    q = q_ref[...]
    t = jnp.where(r == 0, 0, jnp.where(r == nblk - 1, 2, 1))
    nt = (((1,), (1,)), ((), ()))
    s_win = lax.dot_general(q, kw, nt, preferred_element_type=F32) + bias_ref[t]
    s_ctx = lax.dot_general(q, kc_ref[...], nt, preferred_element_type=F32)
    m = jnp.maximum(jnp.max(s_win, axis=-1, keepdims=True), jnp.max(s_ctx, axis=-1, keepdims=True))
    p_win = jnp.exp(s_win - m)
    p_ctx = jnp.exp(s_ctx - m)
    l = jnp.sum(p_win, axis=-1, keepdims=True) + jnp.sum(p_ctx, axis=-1, keepdims=True)
    o = (jnp.dot(p_win.astype(BF16), vw, preferred_element_type=F32)
         + jnp.dot(p_ctx.astype(BF16), vc_ref[...], preferred_element_type=F32))
    o_ref[...] = (o / l).astype(o_ref.dtype)


def _attention(proj, ctx_kv, rpb):
    b, s, _ = proj.shape
    c = ctx_kv.shape[1]
    nh, dh = NA_HEADS, HEAD_DIM
    rows = s // GRID_W
    assert s % GRID_W == 0 and rows % ATT_QROWS == 0 and rows >= ATT_KROWS
    tq = ATT_QROWS * GRID_W
    return pl.pallas_call(
        functools.partial(_attn_kernel, rows=rows),
        grid=(nh, b, rows // ATT_QROWS),
        in_specs=[
            pl.BlockSpec(memory_space=pltpu.SMEM),
            pl.BlockSpec((None, tq, dh), lambda h, bb, r: (bb, r, h)),
            pl.BlockSpec((None, s, dh), lambda h, bb, r: (bb, 0, nh + h)),
            pl.BlockSpec((None, s, dh), lambda h, bb, r: (bb, 0, 2 * nh + h)),
            pl.BlockSpec((None, c, dh), lambda h, bb, r: (bb, 0, h)),
            pl.BlockSpec((None, c, dh), lambda h, bb, r: (bb, 0, nh + h)),
        ],
        out_specs=pl.BlockSpec((None, tq, dh), lambda h, bb, r: (bb, r, h)),
        out_shape=jax.ShapeDtypeStruct((b, s, nh * dh), BF16),
        scratch_shapes=[pltpu.VMEM((3, tq, ATT_KROWS * GRID_W), F32)],
        compiler_params=_params(("arbitrary", "arbitrary", "arbitrary"), 40),
        name="nbr_attention",
    )(rpb.reshape(-1), proj, proj, proj, ctx_kv, ctx_kv)


def _mix_out_kernel(u_ref, gt_ref, ona_ref, x_ref, g_ref, wout_ref, ws_ref, bs_ref, snw_ref,
                    gna_ref, gsg_ref, o_ref, m_ref):
    tm = u_ref.shape[0]
    d_sgu = u_ref.shape[1]
    d_na = ona_ref.shape[1]
    gd = d_sgu // SGU_GROUPS

    @pl.when(pl.program_id(2) == 0)
    def _():
        def body(c, carry):
            r = pl.multiple_of(c * SGU_CHUNK, SGU_CHUNK)
            rs = pl.ds(r, SGU_CHUNK)
            gg = jax.nn.gelu(gt_ref[rs, :].astype(F32))
            ms = jnp.mean(gg * gg, axis=-1, keepdims=True)
            gn = (gg * lax.rsqrt(ms + NORM_EPS) * snw_ref[...]).astype(BF16)
            parts = []
            ssq = jnp.zeros((SGU_CHUNK, 1), F32)
            for g in range(SGU_GROUPS):
                cols = slice(g * gd, (g + 1) * gd)
                mixed = jnp.dot(ws_ref[g], gn[:, cols], preferred_element_type=F32) + bs_ref[g]
                og = jax.nn.gelu(u_ref[rs, cols].astype(F32)) * mixed
                ssq = ssq + jnp.sum(og * og, axis=-1, keepdims=True)
                parts.append(og)
            inv = lax.rsqrt(ssq / d_sgu + NORM_EPS)
            for g in range(SGU_GROUPS):
                cols = slice(g * gd, (g + 1) * gd)
                m_ref[rs, d_na + g * gd:d_na + (g + 1) * gd] = (
                    parts[g] * inv * gsg_ref[:, cols]).astype(BF16)
            on = ona_ref[rs, :].astype(F32)
            ms = jnp.mean(on * on, axis=-1, keepdims=True)
            m_ref[rs, 0:d_na] = (on * lax.rsqrt(ms + NORM_EPS) * gna_ref[...]).astype(BF16)
            return carry
        lax.fori_loop(0, tm // SGU_CHUNK, body, 0)

    acc = jnp.dot(m_ref[...], wout_ref[...], preferred_element_type=F32)
    o_ref[...] = x_ref[...] + g_ref[...] * acc


def _mix_out(proj, o_na, x, gate, w_out, sgu_w, sgu_b, sgu_norm_w, gn_na, gn_sgu, *, tm=512, tn=1024):
    b, s, d = x.shape
    d_na = o_na.shape[2]
    d_sgu = d - d_na
    assert (3 * d_na) % d_sgu == 0
    ub = 3 * d_na // d_sgu
    tm = _tile(s, tm)
    tn = _tile(d, tn)
    assert tm % SGU_CHUNK == 0
    const2 = lambda bb, i, j: (0, 0)
    const3 = lambda bb, i, j: (0, 0, 0)
    return pl.pallas_call(
        _mix_out_kernel,
        grid=(b, s // tm, d // tn),
        in_specs=[
            pl.BlockSpec((None, tm, d_sgu), lambda bb, i, j: (bb, i, ub)),
            pl.BlockSpec((None, tm, d_sgu), lambda bb, i, j: (bb, i, ub + 1)),
            pl.BlockSpec((None, tm, d_na), lambda bb, i, j: (bb, i, 0)),
            pl.BlockSpec((None, tm, tn), lambda bb, i, j: (bb, i, j)),
            pl.BlockSpec((None, 1, tn), lambda bb, i, j: (bb, 0, j)),
            pl.BlockSpec((d, tn), lambda bb, i, j: (0, j)),
            pl.BlockSpec((SGU_GROUPS, SGU_CHUNK, SGU_CHUNK), const3),
            pl.BlockSpec((SGU_GROUPS, SGU_CHUNK, 1), const3),
            pl.BlockSpec((1, d_sgu), const2),
            pl.BlockSpec((1, d_na), const2),
            pl.BlockSpec((1, d_sgu), const2),
        ],
        out_specs=pl.BlockSpec((None, tm, tn), lambda bb, i, j: (bb, i, j)),
        out_shape=jax.ShapeDtypeStruct((b, s, d), F32),
        scratch_shapes=[pltpu.VMEM((tm, d), BF16)],
        compiler_params=_params(("parallel", "parallel", "arbitrary"), 48),
        name="mix_out",
    )(proj, proj, o_na, x, gate, w_out, sgu_w.astype(BF16),
      sgu_b.reshape(SGU_GROUPS, SGU_CHUNK, 1), sgu_norm_w.reshape(1, d_sgu),
      gn_na.reshape(1, d_na), gn_sgu.reshape(1, d_sgu))


def _matmul_residual_kernel(a_ref, w_ref, x_ref, g_ref, o_ref, acc_ref):
    kk = pl.program_id(3)

    @pl.when(kk == 0)
    def _():
        acc_ref[...] = jnp.zeros_like(acc_ref)

    acc_ref[...] += jnp.dot(a_ref[...], w_ref[...], preferred_element_type=F32)

    @pl.when(kk == pl.num_programs(3) - 1)
    def _():
        o_ref[...] = x_ref[...] + g_ref[...] * acc_ref[...]


def _matmul_residual(a, w, x, gate, *, tm=1024, tn=1024, tk=2048):
    b, s, k = a.shape
    n = w.shape[1]
    tm, tn, tk = _tile(s, tm), _tile(n, tn), _tile(k, tk)
    return pl.pallas_call(
        _matmul_residual_kernel,
        grid=(b, s // tm, n // tn, k // tk),
        in_specs=[
            pl.BlockSpec((None, tm, tk), lambda bb, i, j, kk: (bb, i, kk)),
            pl.BlockSpec((tk, tn), lambda bb, i, j, kk: (kk, j)),
            pl.BlockSpec((None, tm, tn), lambda bb, i, j, kk: (bb, i, j)),
            pl.BlockSpec((None, 1, tn), lambda bb, i, j, kk: (bb, 0, j)),
        ],
        out_specs=pl.BlockSpec((None, tm, tn), lambda bb, i, j, kk: (bb, i, j)),
        out_shape=jax.ShapeDtypeStruct((b, s, n), F32),
        scratch_shapes=[pltpu.VMEM((tm, tn), F32)],
        compiler_params=_params(("parallel", "parallel", "parallel", "arbitrary"), 48),
        name="ff2_residual",
    )(a, w, x, gate)


def _rmsnorm_kernel(x_ref, w_ref, o_ref):
    x = x_ref[...]
    ms = jnp.mean(x * x, axis=-1, keepdims=True)
    o_ref[...] = x * lax.rsqrt(ms + NORM_EPS) * w_ref[...]


def _rmsnorm(x, w, *, tm=256):
    b, s, d = x.shape
    tm = _tile(s, tm)
    return pl.pallas_call(
        _rmsnorm_kernel,
        grid=(b, s // tm),
        in_specs=[pl.BlockSpec((None, tm, d), lambda bb, i: (bb, i, 0)),
                  pl.BlockSpec((1, d), lambda bb, i: (0, 0))],
        out_specs=pl.BlockSpec((None, tm, d), lambda bb, i: (bb, i, 0)),
        out_shape=jax.ShapeDtypeStruct((b, s, d), F32),
        compiler_params=_params(("parallel", "parallel"), 32),
        name="final_rmsnorm",
    )(x, w.reshape(1, d))


def kernel(x, c, ctx, c_ctx, w_ada, b_ada, norm1_w, w_in, rpb, sgu_norm_w, sgu_w, sgu_b,
           grp_norm_na, grp_norm_sgu, w_out, norm2_w, w_ff1, w_ff2, final_norm_w):
    b, s, d = x.shape
    n_ctx = ctx.shape[1]
    d_na = NA_HEADS * HEAD_DIM
    depth = w_ada.shape[0]
    assert depth == 1

    n_rows = -(-(b + 1) // 8) * 8
    cvecs = jnp.zeros((n_rows, d), F32).at[:b].set(c).at[b].set(c_ctx)
    mod = _adaln(cvecs, w_ada[0], b_ada[0]).reshape(n_rows, N_MOD, d)
    lat = lambda i: mod[:b, i].reshape(b, 1, d)
    cx = lambda i: mod[b, i].reshape(1, 1, d)

    w_in_b = w_in[0].astype(BF16)
    q_scale = jnp.concatenate([jnp.full((d_na,), HEAD_DIM ** -0.5, F32),
                               jnp.ones((w_in_b.shape[1] - d_na,), F32)])
    proj = _norm_mod_matmul(x, norm1_w[0], lat(1), lat(0), w_in_b, col_scale=q_scale)
    ctx_kv = _norm_mod_matmul(ctx.reshape(1, b * n_ctx, d), norm1_w[0], cx(1), cx(0), w_in_b,
                              n_lo=d_na, n_hi=3 * d_na).reshape(b, n_ctx, 2 * d_na)
    o_na = _attention(proj, ctx_kv, rpb[0])
    x1 = _mix_out(proj, o_na, x, lat(2), w_out[0].astype(BF16), sgu_w[0], sgu_b[0], sgu_norm_w[0],
                  grp_norm_na[0], grp_norm_sgu[0])
    hidden = _norm_mod_matmul(x1, norm2_w[0], lat(4), lat(3), w_ff1[0].astype(BF16), act="relu2")
    x2 = _matmul_residual(hidden, w_ff2[0].astype(BF16), x1, lat(5))
    return _rmsnorm(x2, final_norm_w)
```

```python
import functools
import math

import jax
import jax.numpy as jnp
from jax import lax
from jax.experimental import pallas as pl
from jax.experimental.pallas import tpu as pltpu

F32 = jnp.float32
BF16 = jnp.bfloat16

GRID_W = 64
NA_HEADS = 16
HEAD_DIM = 128
NA_KH_MAX = 8
NA_KW = 16
SGU_GROUPS = 4
SGU_CHUNK = 128
N_MOD = 6
NORM_EPS = 1e-6
MASK_VALUE = -1e30
LOG2E = math.log2(math.e)

ATT_QROWS = 4
ATT_KROWS = 12
ATT_UNROLL = 2
NORM_ROWS = 16
NORM_UNROLL = 4

V7X_VMEM_BYTES = 64 << 20


def _params(semantics, vmem_mib):
    assert (vmem_mib << 20) < V7X_VMEM_BYTES
    return pltpu.CompilerParams(dimension_semantics=semantics, vmem_limit_bytes=vmem_mib << 20)


def _tile(n, pref, align=128, divides=()):
    for t in range(min(pref, n) // align * align, 0, -align):
        if n % t == 0 and all(v % t == 0 for v in divides):
            return t
    assert not any(divides)
    return n


def _adaln_kernel(c_ref, w_ref, b_ref, o_ref):
    c = c_ref[...]
    a = (c * jax.nn.sigmoid(c)).astype(BF16)
    o_ref[...] = jnp.dot(a, w_ref[...].astype(BF16), preferred_element_type=F32) + b_ref[...]


def _adaln(cvecs, w_ada, b_ada):
    r, d = cvecs.shape
    n = w_ada.shape[1]
    tn = _tile(n, 512)
    return pl.pallas_call(
        _adaln_kernel,
        grid=(n // tn,),
        in_specs=[
            pl.BlockSpec((r, d), lambda j: (0, 0)),
            pl.BlockSpec((d, tn), lambda j: (0, j)),
            pl.BlockSpec((1, tn), lambda j: (0, j)),
        ],
        out_specs=pl.BlockSpec((r, tn), lambda j: (0, j)),
        out_shape=jax.ShapeDtypeStruct((r, n), F32),
        compiler_params=_params(("arbitrary",), 40),
        name="adaln",
    )(cvecs, w_ada, b_ada.reshape(1, n))


def _norm_mod_matmul_kernel(x_ref, nw_ref, sc_ref, sh_ref, w_ref, cs_ref, o_ref, h_ref, *,
                            rows_chunk, epilogue, gelu_from):
    tm = x_ref.shape[0]
    j = pl.program_id(2)

    @pl.when(j == 0)
    def _():
        def body(c, carry):
            r = pl.multiple_of(c * rows_chunk, rows_chunk)
            x = x_ref[pl.ds(r, rows_chunk), :]
            ms = jnp.mean(x * x, axis=-1, keepdims=True)
            gain = nw_ref[...] * (1.0 + sc_ref[...])
            h = x * lax.rsqrt(ms + NORM_EPS) * gain + sh_ref[...]
            h_ref[pl.ds(r, rows_chunk), :] = h.astype(BF16)
            return carry
        lax.fori_loop(0, tm // rows_chunk, body, 0, unroll=NORM_UNROLL)

    def matmul():
        return jnp.dot(h_ref[...], w_ref[...], preferred_element_type=F32)

    if epilogue == "plain":
        o_ref[...] = matmul().astype(o_ref.dtype)
    elif epilogue == "relu2":
        o_ref[...] = jnp.square(jnp.maximum(matmul(), 0.0)).astype(o_ref.dtype)
    else:
        assert epilogue == "scale_or_gelu"

        @pl.when(j < gelu_from)
        def _():
            o_ref[...] = (matmul() * cs_ref[...]).astype(o_ref.dtype)

        @pl.when(j >= gelu_from)
        def _():
            o_ref[...] = jax.nn.gelu(matmul()).astype(o_ref.dtype)


def _norm_mod_matmul(x, norm_w, scale, shift, w, *, epilogue="plain", col_scale=None, gelu_cols=0,
                     n_lo=0, n_hi=None, tm=512, tn=1024, vmem_mib=48):
    b, s, k = x.shape
    n_hi = w.shape[1] if n_hi is None else n_hi
    n = n_hi - n_lo
    tm = _tile(s, tm, align=8)
    tn = _tile(n, tn, divides=(n_lo, n - gelu_cols))
    j0 = n_lo // tn
    if col_scale is None:
        col_scale = jnp.ones((n,), F32)
    per_batch = scale.shape[0] == b and b > 1
    mod_map = (lambda bb, i, j: (bb, 0, 0)) if per_batch else (lambda bb, i, j: (0, 0, 0))
    return pl.pallas_call(
        functools.partial(_norm_mod_matmul_kernel, rows_chunk=min(NORM_ROWS, tm), epilogue=epilogue,
                          gelu_from=(n - gelu_cols) // tn),
        grid=(b, s // tm, n // tn),
        in_specs=[
            pl.BlockSpec((None, tm, k), lambda bb, i, j: (bb, i, 0)),
            pl.BlockSpec((1, k), lambda bb, i, j: (0, 0)),
            pl.BlockSpec((None, 1, k), mod_map),
            pl.BlockSpec((None, 1, k), mod_map),
            pl.BlockSpec((k, tn), lambda bb, i, j: (0, j0 + j)),
            pl.BlockSpec((1, tn), lambda bb, i, j: (0, j)),
        ],
        out_specs=pl.BlockSpec((None, tm, tn), lambda bb, i, j: (bb, i, j)),
        out_shape=jax.ShapeDtypeStruct((b, s, n), BF16),
        scratch_shapes=[pltpu.VMEM((tm, k), BF16)],
        compiler_params=_params(("parallel", "parallel", "arbitrary"), vmem_mib),
        name="norm_mod_matmul_" + epilogue,
    )(x, norm_w.reshape(1, k), scale, shift, w, col_scale.reshape(1, n))


def _window_structure(rows, rblk):
    kh = min(NA_KH_MAX, rows)
    kstart = min(max(rblk * ATT_QROWS - kh // 2, 0), rows - ATT_KROWS)
    out = []
    for i in range(ATT_QROWS):
        qr = rblk * ATT_QROWS + i
        r0 = min(max(qr - kh // 2, 0), rows - kh)
        row = []
        for j in range(ATT_KROWS):
            kr = kstart + j
            row.append(kr - qr + NA_KH_MAX - 1 if r0 <= kr < r0 + kh else None)
        assert sum(e is not None for e in row) == kh
        out.append(tuple(row))
    return tuple(out)


def _block_types(rows):
    nblk = rows // ATT_QROWS
    reps = [0, min(1, nblk - 1), nblk - 1]
    structs = [_window_structure(rows, r) for r in reps]
    for r in range(nblk):
        t = 0 if r == 0 else (2 if r == nblk - 1 else 1)
        assert _window_structure(rows, r) == structs[t]
    return structs


def _attn_kernel(rpb_ref, q_ref, k_ref, v_ref, kc_ref, vc_ref, *rest, rows, n_cast):
    cast_in, o_ref, cast_out, bias_ref = (rest[:n_cast], rest[n_cast], rest[n_cast + 1:-1], rest[-1])
    for src, dst in zip(cast_in, cast_out):
        dst[...] = src[...].astype(dst.dtype)

    w = GRID_W
    n_ro = 2 * NA_KH_MAX - 1
    n_co = 2 * NA_KW - 1
    nblk = rows // ATT_QROWS
    kh = min(NA_KH_MAX, rows)
    tq = ATT_QROWS * w
    tk = ATT_KROWS * w
    h = pl.program_id(0)
    b = pl.program_id(1)

    @pl.when(b == 0)
    def _():
        qc = lax.broadcasted_iota(jnp.int32, (w, 2 * w), 0)
        lane = lax.broadcasted_iota(jnp.int32, (w, 2 * w), 1)
        kc = lane % w
        cs = jnp.clip(qc - NA_KW // 2, 0, w - NA_KW)
        col_ok = (kc >= cs) & (kc < cs + NA_KW)
        diff = kc - qc + (NA_KW - 1)
        hit = [(diff == d) & col_ok for d in range(n_co)]
        base = h * (n_ro * n_co)
        neg = jnp.full((w, 2 * w), MASK_VALUE, F32)
        tables = []
        for ro in range(n_ro):
            acc = neg
            for d in range(n_co):
                acc = jnp.where(hit[d], rpb_ref[base + ro * n_co + d] * LOG2E, acc)
            tables.append(acc)
        left = lane < w
        for t, struct in enumerate(_block_types(rows)):
            for i in range(ATT_QROWS):
                for jp in range(ATT_KROWS // 2):
                    ro_l, ro_r = struct[i][2 * jp], struct[i][2 * jp + 1]
                    blk_l = neg if ro_l is None else tables[ro_l]
                    blk_r = neg if ro_r is None else tables[ro_r]
                    blk = blk_l if ro_l == ro_r else jnp.where(left, blk_l, blk_r)
                    bias_ref[t, i * w:(i + 1) * w, jp * 2 * w:(jp + 1) * 2 * w] = blk

    nt = (((1,), (1,)), ((), ()))

    def block(r):
        kstart = jnp.clip(r * ATT_QROWS - kh // 2, 0, rows - ATT_KROWS)
        ks = pl.multiple_of(kstart * w, w)
        qs = pl.multiple_of(r * tq, tq)
        q = q_ref[pl.ds(qs, tq), :]
        kw = k_ref[pl.ds(ks, tk), :]
        vw = v_ref[pl.ds(ks, tk), :]
        t = jnp.where(r == 0, 0, jnp.where(r == nblk - 1, 2, 1))
        s_win = lax.dot_general(q, kw, nt, preferred_element_type=F32) + bias_ref[t]
        s_ctx = lax.dot_general(q, kc_ref[...], nt, preferred_element_type=F32)
        m = jnp.maximum(jnp.max(s_win, axis=-1, keepdims=True),
                        jnp.max(s_ctx, axis=-1, keepdims=True))
        p_win = jnp.exp2(s_win - m)
        p_ctx = jnp.exp2(s_ctx - m)
        l = jnp.sum(p_win, axis=-1, keepdims=True) + jnp.sum(p_ctx, axis=-1, keepdims=True)
        o = (jnp.dot(p_win.astype(BF16), vw, preferred_element_type=F32)
             + jnp.dot(p_ctx.astype(BF16), vc_ref[...], preferred_element_type=F32))
        o_ref[pl.ds(qs, tq), :] = (o * (1.0 / l)).astype(o_ref.dtype)

    def body(it, carry):
        for u in range(ATT_UNROLL):
            block(it * ATT_UNROLL + u)
        return carry
    lax.fori_loop(0, nblk // ATT_UNROLL, body, 0)


def _attention(proj, ctx_kv, rpb, cast_weights):
    b, s, _ = proj.shape
    c = ctx_kv.shape[1]
    nh, dh = NA_HEADS, HEAD_DIM
    rows = s // GRID_W
    assert s % GRID_W == 0 and rows % (ATT_QROWS * ATT_UNROLL) == 0 and rows >= ATT_KROWS
    steps = nh * b
    cast_specs = []
    for wgt in cast_weights:
        slab = wgt.shape[0] // steps
        assert wgt.shape[0] % steps == 0 and slab % 16 == 0
        cast_specs.append(pl.BlockSpec((slab, wgt.shape[1]), lambda h, bb: (h * b + bb, 0)))
    return pl.pallas_call(
        functools.partial(_attn_kernel, rows=rows, n_cast=len(cast_weights)),
        grid=(nh, b),
        in_specs=[
            pl.BlockSpec(memory_space=pltpu.SMEM),
            pl.BlockSpec((None, s, dh), lambda h, bb: (bb, 0, h)),
            pl.BlockSpec((None, s, dh), lambda h, bb: (bb, 0, nh + h)),
            pl.BlockSpec((None, s, dh), lambda h, bb: (bb, 0, 2 * nh + h)),
            pl.BlockSpec((None, c, dh), lambda h, bb: (bb, 0, h)),
            pl.BlockSpec((None, c, dh), lambda h, bb: (bb, 0, nh + h)),
        ] + cast_specs,
        out_specs=[pl.BlockSpec((None, s, dh), lambda h, bb: (bb, 0, h))] + cast_specs,
        out_shape=[jax.ShapeDtypeStruct((b, s, nh * dh), BF16)]
        + [jax.ShapeDtypeStruct(wgt.shape, BF16) for wgt in cast_weights],
        scratch_shapes=[pltpu.VMEM((3, ATT_QROWS * GRID_W, ATT_KROWS * GRID_W), F32)],
        compiler_params=_params(("arbitrary", "arbitrary"), 56),
        name="nbr_attention",
    )(rpb.reshape(-1), proj, proj, proj, ctx_kv, ctx_kv, *cast_weights)


def _mix_out_kernel(u_ref, g_ref, ona_ref, x_ref, gate_ref, wout_ref, ws_ref, bs_ref, snw_ref,
                    gna_ref, gsg_ref, o_ref, m_ref):
    tm = u_ref.shape[0]
    d_sgu = u_ref.shape[1]
    d_na = ona_ref.shape[1]
    gd = d_sgu // SGU_GROUPS

    @pl.when(pl.program_id(2) == 0)
    def _():
        def body(c, carry):
            r = pl.multiple_of(c * SGU_CHUNK, SGU_CHUNK)
            rs = pl.ds(r, SGU_CHUNK)
            gg = g_ref[rs, :].astype(F32)
            ms = jnp.mean(gg * gg, axis=-1, keepdims=True)
            gn = (gg * lax.rsqrt(ms + NORM_EPS) * snw_ref[...]).astype(BF16)
            parts = []
            ssq = jnp.zeros((SGU_CHUNK, 1), F32)
            for g in range(SGU_GROUPS):
                cols = slice(g * gd, (g + 1) * gd)
                mixed = jnp.dot(ws_ref[g], gn[:, cols], preferred_element_type=F32) + bs_ref[g]
                og = u_ref[rs, cols].astype(F32) * mixed
                ssq = ssq + jnp.sum(og * og, axis=-1, keepdims=True)
                parts.append(og)
            inv = lax.rsqrt(ssq / d_sgu + NORM_EPS)
            for g in range(SGU_GROUPS):
                cols = slice(g * gd, (g + 1) * gd)
                m_ref[rs, d_na + g * gd:d_na + (g + 1) * gd] = (
                    parts[g] * inv * gsg_ref[:, cols]).astype(BF16)
            on = ona_ref[rs, :].astype(F32)
            ms = jnp.mean(on * on, axis=-1, keepdims=True)
            m_ref[rs, 0:d_na] = (on * lax.rsqrt(ms + NORM_EPS) * gna_ref[...]).astype(BF16)
            return carry
        lax.fori_loop(0, tm // SGU_CHUNK, body, 0)

    acc = jnp.dot(m_ref[...], wout_ref[...], preferred_element_type=F32)
    o_ref[...] = x_ref[...] + gate_ref[...] * acc


def _mix_out(proj, o_na, x, gate, w_out, sgu_w, sgu_b, sgu_norm_w, gn_na, gn_sgu, *, tm=512, tn=1024):
    b, s, d = x.shape
    d_na = o_na.shape[2]
    d_sgu = d - d_na
    assert (3 * d_na) % d_sgu == 0
    ub = 3 * d_na // d_sgu
    tm = _tile(s, tm)
    tn = _tile(d, tn)
    assert tm % SGU_CHUNK == 0
    const2 = lambda bb, i, j: (0, 0)
    const3 = lambda bb, i, j: (0, 0, 0)
    return pl.pallas_call(
        _mix_out_kernel,
        grid=(b, s // tm, d // tn),
        in_specs=[
            pl.BlockSpec((None, tm, d_sgu), lambda bb, i, j: (bb, i, ub)),
            pl.BlockSpec((None, tm, d_sgu), lambda bb, i, j: (bb, i, ub + 1)),
            pl.BlockSpec((None, tm, d_na), lambda bb, i, j: (bb, i, 0)),
            pl.BlockSpec((None, tm, tn), lambda bb, i, j: (bb, i, j)),
            pl.BlockSpec((None, 1, tn), lambda bb, i, j: (bb, 0, j)),
            pl.BlockSpec((d, tn), lambda bb, i, j: (0, j)),
            pl.BlockSpec((SGU_GROUPS, SGU_CHUNK, SGU_CHUNK), const3),
            pl.BlockSpec((SGU_GROUPS, SGU_CHUNK, 1), const3),
            pl.BlockSpec((1, d_sgu), const2),
            pl.BlockSpec((1, d_na), const2),
            pl.BlockSpec((1, d_sgu), const2),
        ],
        out_specs=pl.BlockSpec((None, tm, tn), lambda bb, i, j: (bb, i, j)),
        out_shape=jax.ShapeDtypeStruct((b, s, d), F32),
        scratch_shapes=[pltpu.VMEM((tm, d), BF16)],
        compiler_params=_params(("parallel", "parallel", "arbitrary"), 48),
        name="mix_out",
    )(proj, proj, o_na, x, gate, w_out, sgu_w.astype(BF16),
      sgu_b.reshape(SGU_GROUPS, SGU_CHUNK, 1), sgu_norm_w.reshape(1, d_sgu),
      gn_na.reshape(1, d_na), gn_sgu.reshape(1, d_sgu))


def _matmul_residual_kernel(a_ref, w_ref, x_ref, g_ref, o_ref, acc_ref):
    kk = pl.program_id(3)

    @pl.when(kk == 0)
    def _():
        acc_ref[...] = jnp.zeros_like(acc_ref)

    acc_ref[...] += jnp.dot(a_ref[...], w_ref[...], preferred_element_type=F32)

    @pl.when(kk == pl.num_programs(3) - 1)
    def _():
        o_ref[...] = x_ref[...] + g_ref[...] * acc_ref[...]


def _matmul_residual(a, w, x, gate, *, tm=1024, tn=1024, tk=2048):
    b, s, k = a.shape
    n = w.shape[1]
    tm, tn, tk = _tile(s, tm), _tile(n, tn), _tile(k, tk)
    return pl.pallas_call(
        _matmul_residual_kernel,
        grid=(b, s // tm, n // tn, k // tk),
        in_specs=[
            pl.BlockSpec((None, tm, tk), lambda bb, i, j, kk: (bb, i, kk)),
            pl.BlockSpec((tk, tn), lambda bb, i, j, kk: (kk, j)),
            pl.BlockSpec((None, tm, tn), lambda bb, i, j, kk: (bb, i, j)),
            pl.BlockSpec((None, 1, tn), lambda bb, i, j, kk: (bb, 0, j)),
        ],
        out_specs=pl.BlockSpec((None, tm, tn), lambda bb, i, j, kk: (bb, i, j)),
        out_shape=jax.ShapeDtypeStruct((b, s, n), F32),
        scratch_shapes=[pltpu.VMEM((tm, tn), F32)],
        compiler_params=_params(("parallel", "parallel", "parallel", "arbitrary"), 48),
        name="ff2_residual",
    )(a, w, x, gate)


def _rmsnorm_kernel(x_ref, w_ref, o_ref):
    x = x_ref[...]
    ms = jnp.mean(x * x, axis=-1, keepdims=True)
    o_ref[...] = x * lax.rsqrt(ms + NORM_EPS) * w_ref[...]


def _rmsnorm(x, w, *, tm=256):
    b, s, d = x.shape
    tm = _tile(s, tm)
    return pl.pallas_call(
        _rmsnorm_kernel,
        grid=(b, s // tm),
        in_specs=[pl.BlockSpec((None, tm, d), lambda bb, i: (bb, i, 0)),
                  pl.BlockSpec((1, d), lambda bb, i: (0, 0))],
        out_specs=pl.BlockSpec((None, tm, d), lambda bb, i: (bb, i, 0)),
        out_shape=jax.ShapeDtypeStruct((b, s, d), F32),
        compiler_params=_params(("parallel", "parallel"), 32),
        name="final_rmsnorm",
    )(x, w.reshape(1, d))


def kernel(x, c, ctx, c_ctx, w_ada, b_ada, norm1_w, w_in, rpb, sgu_norm_w, sgu_w, sgu_b,
           grp_norm_na, grp_norm_sgu, w_out, norm2_w, w_ff1, w_ff2, final_norm_w):
    b, s, d = x.shape
    n_ctx = ctx.shape[1]
    d_na = NA_HEADS * HEAD_DIM
    d_sgu = d - d_na
    depth = w_ada.shape[0]
    assert depth == 1

    n_rows = -(-(b + 1) // 8) * 8
    cvecs = jnp.zeros((n_rows, d), F32).at[:b].set(c).at[b].set(c_ctx)
    mod = _adaln(cvecs, w_ada[0], b_ada[0]).reshape(n_rows, N_MOD, d)
    lat = lambda i: mod[:b, i].reshape(b, 1, d)
    cx = lambda i: mod[b, i].reshape(1, 1, d)

    w_in_b = w_in[0].astype(BF16)
    qkv_scale = jnp.concatenate([jnp.full((d_na,), HEAD_DIM ** -0.5 * LOG2E, F32),
                                 jnp.ones((2 * d_na + 2 * d_sgu,), F32)])
    proj = _norm_mod_matmul(x, norm1_w[0], lat(1), lat(0), w_in_b, epilogue="scale_or_gelu",
                            col_scale=qkv_scale, gelu_cols=2 * d_sgu)
    ctx_kv = _norm_mod_matmul(ctx.reshape(1, b * n_ctx, d), norm1_w[0], cx(1), cx(0), w_in_b,
                              n_lo=d_na, n_hi=3 * d_na).reshape(b, n_ctx, 2 * d_na)
    o_na, w_out_b, w_ff1_b, w_ff2_b = _attention(proj, ctx_kv, rpb[0], [w_out[0], w_ff1[0], w_ff2[0]])
    x1 = _mix_out(proj, o_na, x, lat(2), w_out_b, sgu_w[0], sgu_b[0], sgu_norm_w[0],
                  grp_norm_na[0], grp_norm_sgu[0])
    hidden = _norm_mod_matmul(x1, norm2_w[0], lat(4), lat(3), w_ff1_b, epilogue="relu2")
    x2 = _matmul_residual(hidden, w_ff2_b, x1, lat(5))
    return _rmsnorm(x2, final_norm_w)
```

```python
import functools
import math

import jax
import jax.numpy as jnp
from jax import lax
from jax.experimental import pallas as pl
from jax.experimental.pallas import tpu as pltpu

F32 = jnp.float32
BF16 = jnp.bfloat16

GRID_W = 64
NA_HEADS = 16
HEAD_DIM = 128
NA_KH_MAX = 8
NA_KW = 16
SGU_GROUPS = 4
SGU_CHUNK = 128
N_MOD = 6
NORM_EPS = 1e-6
MASK_VALUE = -1e30
LOG2E = math.log2(math.e)

ATT_QROWS = 4
ATT_KROWS = 12
ATT_UNROLL = 2
NORM_ROWS = 16
NORM_UNROLL = 4

V7X_VMEM_BYTES = 64 << 20


def _params(semantics, vmem_mib):
    assert (vmem_mib << 20) < V7X_VMEM_BYTES
    return pltpu.CompilerParams(dimension_semantics=semantics, vmem_limit_bytes=vmem_mib << 20)


def _tile(n, pref, align=128, divides=()):
    for t in range(min(pref, n) // align * align, 0, -align):
        if n % t == 0 and all(v % t == 0 for v in divides):
            return t
    assert not any(divides)
    return n


def _adaln_kernel(c_ref, w_ref, b_ref, o_ref):
    c = c_ref[...]
    a = (c * jax.nn.sigmoid(c)).astype(BF16)
    o_ref[...] = jnp.dot(a, w_ref[...].astype(BF16), preferred_element_type=F32) + b_ref[...]


def _adaln(cvecs, w_ada, b_ada):
    r, d = cvecs.shape
    n = w_ada.shape[1]
    tn = _tile(n, 512)
    return pl.pallas_call(
        _adaln_kernel,
        grid=(n // tn,),
        in_specs=[
            pl.BlockSpec((r, d), lambda j: (0, 0)),
            pl.BlockSpec((d, tn), lambda j: (0, j)),
            pl.BlockSpec((1, tn), lambda j: (0, j)),
        ],
        out_specs=pl.BlockSpec((r, tn), lambda j: (0, j)),
        out_shape=jax.ShapeDtypeStruct((r, n), F32),
        compiler_params=_params(("arbitrary",), 40),
        name="adaln",
    )(cvecs, w_ada, b_ada.reshape(1, n))


def _norm_mod_matmul_kernel(x_hbm, nw_ref, sc_ref, sh_ref, w_ref, cs_ref, o_ref, x_buf, h_ref, x_sem, *,
                            rows_chunk, epilogue, gelu_from):
    tm = x_buf.shape[0]
    bb = pl.program_id(0)
    i = pl.program_id(1)
    j = pl.program_id(2)
    n_i = pl.num_programs(1)
    n_tiles = pl.num_programs(0) * n_i

    def x_copy(tile):
        return pltpu.make_async_copy(
            x_hbm.at[tile // n_i, pl.ds((tile % n_i) * tm, tm), :], x_buf, x_sem)

    @pl.when(j == 0)
    def _():
        tile = bb * n_i + i

        @pl.when(tile == 0)
        def _():
            x_copy(tile).start()

        x_copy(tile).wait()

        def body(c, carry):
            r = pl.multiple_of(c * rows_chunk, rows_chunk)
            x = x_buf[pl.ds(r, rows_chunk), :]
            ms = jnp.mean(x * x, axis=-1, keepdims=True)
            gain = nw_ref[...] * (1.0 + sc_ref[...])
            h = x * lax.rsqrt(ms + NORM_EPS) * gain + sh_ref[...]
            h_ref[pl.ds(r, rows_chunk), :] = h.astype(BF16)
            return carry
        lax.fori_loop(0, tm // rows_chunk, body, 0, unroll=NORM_UNROLL)

        @pl.when(tile + 1 < n_tiles)
        def _():
            x_copy(tile + 1).start()

    def matmul():
        return jnp.dot(h_ref[...], w_ref[...], preferred_element_type=F32)

    if epilogue == "plain":
        o_ref[...] = matmul().astype(o_ref.dtype)
    elif epilogue == "relu2":
        o_ref[...] = jnp.square(jnp.maximum(matmul(), 0.0)).astype(o_ref.dtype)
    else:
        assert epilogue == "scale_or_gelu"

        @pl.when(j < gelu_from)
        def _():
            o_ref[...] = (matmul() * cs_ref[...]).astype(o_ref.dtype)

        @pl.when(j >= gelu_from)
        def _():
            o_ref[...] = jax.nn.gelu(matmul()).astype(o_ref.dtype)


def _norm_mod_matmul(x, norm_w, scale, shift, w, *, epilogue="plain", col_scale=None, gelu_cols=0,
                     n_lo=0, n_hi=None, tm=1024, tn=1024, vmem_mib=56):
    b, s, k = x.shape
    n_hi = w.shape[1] if n_hi is None else n_hi
    n = n_hi - n_lo
    tm = _tile(s, tm, align=8)
    tn = _tile(n, tn, divides=(n_lo, n - gelu_cols))
    j0 = n_lo // tn
    if col_scale is None:
        col_scale = jnp.ones((n,), F32)
    per_batch = scale.shape[0] == b and b > 1
    mod_map = (lambda bb, i, j: (bb, 0, 0)) if per_batch else (lambda bb, i, j: (0, 0, 0))
    return pl.pallas_call(
        functools.partial(_norm_mod_matmul_kernel, rows_chunk=min(NORM_ROWS, tm), epilogue=epilogue,
                          gelu_from=(n - gelu_cols) // tn),
        grid=(b, s // tm, n // tn),
        in_specs=[
            pl.BlockSpec(memory_space=pl.ANY),
            pl.BlockSpec((1, k), lambda bb, i, j: (0, 0)),
            pl.BlockSpec((None, 1, k), mod_map),
            pl.BlockSpec((None, 1, k), mod_map),
            pl.BlockSpec((k, tn), lambda bb, i, j: (0, j0 + j)),
            pl.BlockSpec((1, tn), lambda bb, i, j: (0, j)),
        ],
        out_specs=pl.BlockSpec((None, tm, tn), lambda bb, i, j: (bb, i, j)),
        out_shape=jax.ShapeDtypeStruct((b, s, n), BF16),
        scratch_shapes=[pltpu.VMEM((tm, k), F32), pltpu.VMEM((tm, k), BF16),
                        pltpu.SemaphoreType.DMA(())],
        compiler_params=_params(("arbitrary", "arbitrary", "arbitrary"), vmem_mib),
        name="norm_mod_matmul_" + epilogue,
    )(x, norm_w.reshape(1, k), scale, shift, w, col_scale.reshape(1, n))


def _window_structure(rows, rblk):
    kh = min(NA_KH_MAX, rows)
    kstart = min(max(rblk * ATT_QROWS - kh // 2, 0), rows - ATT_KROWS)
    out = []
    for i in range(ATT_QROWS):
        qr = rblk * ATT_QROWS + i
        r0 = min(max(qr - kh // 2, 0), rows - kh)
        row = []
        for j in range(ATT_KROWS):
            kr = kstart + j
            row.append(kr - qr + NA_KH_MAX - 1 if r0 <= kr < r0 + kh else None)
        assert sum(e is not None for e in row) == kh
        out.append(tuple(row))
    return tuple(out)


def _block_types(rows):
    nblk = rows // ATT_QROWS
    reps = [0, min(1, nblk - 1), nblk - 1]
    structs = [_window_structure(rows, r) for r in reps]
    for r in range(nblk):
        t = 0 if r == 0 else (2 if r == nblk - 1 else 1)
        assert _window_structure(rows, r) == structs[t]
    return structs


def _attn_kernel(rpb_ref, q_ref, k_ref, v_ref, kc_ref, vc_ref, *rest, rows, n_cast):
    cast_in, o_ref, cast_out, bias_ref = (rest[:n_cast], rest[n_cast], rest[n_cast + 1:-1], rest[-1])
    for src, dst in zip(cast_in, cast_out):
        dst[...] = src[...].astype(dst.dtype)

    w = GRID_W
    n_ro = 2 * NA_KH_MAX - 1
    n_co = 2 * NA_KW - 1
    nblk = rows // ATT_QROWS
    kh = min(NA_KH_MAX, rows)
    tq = ATT_QROWS * w
    tk = ATT_KROWS * w
    h = pl.program_id(0)
    b = pl.program_id(1)

    @pl.when(b == 0)
    def _():
        qc = lax.broadcasted_iota(jnp.int32, (w, 2 * w), 0)
        lane = lax.broadcasted_iota(jnp.int32, (w, 2 * w), 1)
        kc = lane % w
        cs = jnp.clip(qc - NA_KW // 2, 0, w - NA_KW)
        col_ok = (kc >= cs) & (kc < cs + NA_KW)
        diff = kc - qc + (NA_KW - 1)
        hit = [(diff == d) & col_ok for d in range(n_co)]
        base = h * (n_ro * n_co)
        neg = jnp.full((w, 2 * w), MASK_VALUE, F32)
        tables = []
        for ro in range(n_ro):
            acc = neg
            for d in range(n_co):
                acc = jnp.where(hit[d], rpb_ref[base + ro * n_co + d] * LOG2E, acc)
            tables.append(acc)
        left = lane < w
        for t, struct in enumerate(_block_types(rows)):
            for i in range(ATT_QROWS):
                for jp in range(ATT_KROWS // 2):
                    ro_l, ro_r = struct[i][2 * jp], struct[i][2 * jp + 1]
                    blk_l = neg if ro_l is None else tables[ro_l]
                    blk_r = neg if ro_r is None else tables[ro_r]
                    blk = blk_l if ro_l == ro_r else jnp.where(left, blk_l, blk_r)
                    bias_ref[t, i * w:(i + 1) * w, jp * 2 * w:(jp + 1) * 2 * w] = blk

    nt = (((1,), (1,)), ((), ()))

    def block(r):
        kstart = jnp.clip(r * ATT_QROWS - kh // 2, 0, rows - ATT_KROWS)
        ks = pl.multiple_of(kstart * w, w)
        qs = pl.multiple_of(r * tq, tq)
        q = q_ref[pl.ds(qs, tq), :]
        kw = k_ref[pl.ds(ks, tk), :]
        vw = v_ref[pl.ds(ks, tk), :]
        t = jnp.where(r == 0, 0, jnp.where(r == nblk - 1, 2, 1))
        s_win = lax.dot_general(q, kw, nt, preferred_element_type=F32) + bias_ref[t]
        s_ctx = lax.dot_general(q, kc_ref[...], nt, preferred_element_type=F32)
        m = jnp.maximum(jnp.max(s_win, axis=-1, keepdims=True),
                        jnp.max(s_ctx, axis=-1, keepdims=True))
        p_win = jnp.exp2(s_win - m)
        p_ctx = jnp.exp2(s_ctx - m)
        l = jnp.sum(p_win, axis=-1, keepdims=True) + jnp.sum(p_ctx, axis=-1, keepdims=True)
        o = (jnp.dot(p_win.astype(BF16), vw, preferred_element_type=F32)
             + jnp.dot(p_ctx.astype(BF16), vc_ref[...], preferred_element_type=F32))
        o_ref[pl.ds(qs, tq), :] = (o * (1.0 / l)).astype(o_ref.dtype)

    def body(it, carry):
        for u in range(ATT_UNROLL):
            block(it * ATT_UNROLL + u)
        return carry
    lax.fori_loop(0, nblk // ATT_UNROLL, body, 0)


def _attention(proj, ctx_kv, rpb, cast_weights):
    b, s, _ = proj.shape
    c = ctx_kv.shape[1]
    nh, dh = NA_HEADS, HEAD_DIM
    rows = s // GRID_W
    assert s % GRID_W == 0 and rows % (ATT_QROWS * ATT_UNROLL) == 0 and rows >= ATT_KROWS
    steps = nh * b
    cast_specs = []
    for wgt in cast_weights:
        slab = wgt.shape[0] // steps
        assert wgt.shape[0] % steps == 0 and slab % 16 == 0
        cast_specs.append(pl.BlockSpec((slab, wgt.shape[1]), lambda h, bb: (h * b + bb, 0)))
    return pl.pallas_call(
        functools.partial(_attn_kernel, rows=rows, n_cast=len(cast_weights)),
        grid=(nh, b),
        in_specs=[
            pl.BlockSpec(memory_space=pltpu.SMEM),
            pl.BlockSpec((None, s, dh), lambda h, bb: (bb, 0, h)),
            pl.BlockSpec((None, s, dh), lambda h, bb: (bb, 0, nh + h)),
            pl.BlockSpec((None, s, dh), lambda h, bb: (bb, 0, 2 * nh + h)),
            pl.BlockSpec((None, c, dh), lambda h, bb: (bb, 0, h)),
            pl.BlockSpec((None, c, dh), lambda h, bb: (bb, 0, nh + h)),
        ] + cast_specs,
        out_specs=[pl.BlockSpec((None, s, dh), lambda h, bb: (bb, 0, h))] + cast_specs,
        out_shape=[jax.ShapeDtypeStruct((b, s, nh * dh), BF16)]
        + [jax.ShapeDtypeStruct(wgt.shape, BF16) for wgt in cast_weights],
        scratch_shapes=[pltpu.VMEM((3, ATT_QROWS * GRID_W, ATT_KROWS * GRID_W), F32)],
        compiler_params=_params(("arbitrary", "arbitrary"), 56),
        name="nbr_attention",
    )(rpb.reshape(-1), proj, proj, proj, ctx_kv, ctx_kv, *cast_weights)


def _mix_out_kernel(u_ref, g_ref, ona_ref, x_ref, gate_ref, wout_ref, ws_ref, bs_ref, snw_ref,
                    gna_ref, gsg_ref, o_ref, m_ref):
    tm = u_ref.shape[0]
    d_sgu = u_ref.shape[1]
    d_na = ona_ref.shape[1]
    gd = d_sgu // SGU_GROUPS

    @pl.when(pl.program_id(2) == 0)
    def _():
        def body(c, carry):
            r = pl.multiple_of(c * SGU_CHUNK, SGU_CHUNK)
            rs = pl.ds(r, SGU_CHUNK)
            gg = g_ref[rs, :].astype(F32)
            ms = jnp.mean(gg * gg, axis=-1, keepdims=True)
            gn = (gg * lax.rsqrt(ms + NORM_EPS) * snw_ref[...]).astype(BF16)
            parts = []
            ssq = jnp.zeros((SGU_CHUNK, 1), F32)
            for g in range(SGU_GROUPS):
                cols = slice(g * gd, (g + 1) * gd)
                mixed = jnp.dot(ws_ref[g], gn[:, cols], preferred_element_type=F32) + bs_ref[g]
                og = u_ref[rs, cols].astype(F32) * mixed
                ssq = ssq + jnp.sum(og * og, axis=-1, keepdims=True)
                parts.append(og)
            inv = lax.rsqrt(ssq / d_sgu + NORM_EPS)
            for g in range(SGU_GROUPS):
                cols = slice(g * gd, (g + 1) * gd)
                m_ref[rs, d_na + g * gd:d_na + (g + 1) * gd] = (
                    parts[g] * inv * gsg_ref[:, cols]).astype(BF16)
            on = ona_ref[rs, :].astype(F32)
            ms = jnp.mean(on * on, axis=-1, keepdims=True)
            m_ref[rs, 0:d_na] = (on * lax.rsqrt(ms + NORM_EPS) * gna_ref[...]).astype(BF16)
            return carry
        lax.fori_loop(0, tm // SGU_CHUNK, body, 0)

    acc = jnp.dot(m_ref[...], wout_ref[...], preferred_element_type=F32)
    o_ref[...] = x_ref[...] + gate_ref[...] * acc


def _mix_out(proj, o_na, x, gate, w_out, sgu_w, sgu_b, sgu_norm_w, gn_na, gn_sgu, *, tm=512, tn=1024):
    b, s, d = x.shape
    d_na = o_na.shape[2]
    d_sgu = d - d_na
    assert (3 * d_na) % d_sgu == 0
    ub = 3 * d_na // d_sgu
    tm = _tile(s, tm)
    tn = _tile(d, tn)
    assert tm % SGU_CHUNK == 0
    const2 = lambda bb, i, j: (0, 0)
    const3 = lambda bb, i, j: (0, 0, 0)
    return pl.pallas_call(
        _mix_out_kernel,
        grid=(b, s // tm, d // tn),
        in_specs=[
            pl.BlockSpec((None, tm, d_sgu), lambda bb, i, j: (bb, i, ub)),
            pl.BlockSpec((None, tm, d_sgu), lambda bb, i, j: (bb, i, ub + 1)),
            pl.BlockSpec((None, tm, d_na), lambda bb, i, j: (bb, i, 0)),
            pl.BlockSpec((None, tm, tn), lambda bb, i, j: (bb, i, j)),
            pl.BlockSpec((None, 1, tn), lambda bb, i, j: (bb, 0, j)),
            pl.BlockSpec((d, tn), lambda bb, i, j: (0, j)),
            pl.BlockSpec((SGU_GROUPS, SGU_CHUNK, SGU_CHUNK), const3),
            pl.BlockSpec((SGU_GROUPS, SGU_CHUNK, 1), const3),
            pl.BlockSpec((1, d_sgu), const2),
            pl.BlockSpec((1, d_na), const2),
            pl.BlockSpec((1, d_sgu), const2),
        ],
        out_specs=pl.BlockSpec((None, tm, tn), lambda bb, i, j: (bb, i, j)),
        out_shape=jax.ShapeDtypeStruct((b, s, d), F32),
        scratch_shapes=[pltpu.VMEM((tm, d), BF16)],
        compiler_params=_params(("parallel", "parallel", "arbitrary"), 48),
        name="mix_out",
    )(proj, proj, o_na, x, gate, w_out, sgu_w.astype(BF16),
      sgu_b.reshape(SGU_GROUPS, SGU_CHUNK, 1), sgu_norm_w.reshape(1, d_sgu),
      gn_na.reshape(1, d_na), gn_sgu.reshape(1, d_sgu))


def _matmul_residual_kernel(a_ref, w_ref, x_ref, g_ref, o_ref, acc_ref):
    kk = pl.program_id(3)
    last = pl.num_programs(3) - 1

    def partial_product():
        return jnp.dot(a_ref[...], w_ref[...], preferred_element_type=F32)

    @pl.when(kk == 0)
    def _():
        acc_ref[...] = partial_product()

    @pl.when((kk > 0) & (kk < last))
    def _():
        acc_ref[...] += partial_product()

    @pl.when(kk == last)
    def _():
        o_ref[...] = x_ref[...] + g_ref[...] * (acc_ref[...] + partial_product())


def _matmul_residual(a, w, x, gate, *, tm=1024, tn=1024, tk=2048):
    b, s, k = a.shape
    n = w.shape[1]
    tm, tn, tk = _tile(s, tm), _tile(n, tn), _tile(k, tk)
    assert k // tk >= 2
    return pl.pallas_call(
        _matmul_residual_kernel,
        grid=(b, s // tm, n // tn, k // tk),
        in_specs=[
            pl.BlockSpec((None, tm, tk), lambda bb, i, j, kk: (bb, i, kk)),
            pl.BlockSpec((tk, tn), lambda bb, i, j, kk: (kk, j)),
            pl.BlockSpec((None, tm, tn), lambda bb, i, j, kk: (bb, i, j)),
            pl.BlockSpec((None, 1, tn), lambda bb, i, j, kk: (bb, 0, j)),
        ],
        out_specs=pl.BlockSpec((None, tm, tn), lambda bb, i, j, kk: (bb, i, j)),
        out_shape=jax.ShapeDtypeStruct((b, s, n), F32),
        scratch_shapes=[pltpu.VMEM((tm, tn), F32)],
        compiler_params=_params(("parallel", "parallel", "parallel", "arbitrary"), 48),
        name="ff2_residual",
    )(a, w, x, gate)


def _rmsnorm_kernel(x_ref, w_ref, o_ref):
    x = x_ref[...]
    ms = jnp.mean(x * x, axis=-1, keepdims=True)
    o_ref[...] = x * lax.rsqrt(ms + NORM_EPS) * w_ref[...]


def _rmsnorm(x, w, *, tm=256):
    b, s, d = x.shape
    tm = _tile(s, tm)
    return pl.pallas_call(
        _rmsnorm_kernel,
        grid=(b, s // tm),
        in_specs=[pl.BlockSpec((None, tm, d), lambda bb, i: (bb, i, 0)),
                  pl.BlockSpec((1, d), lambda bb, i: (0, 0))],
        out_specs=pl.BlockSpec((None, tm, d), lambda bb, i: (bb, i, 0)),
        out_shape=jax.ShapeDtypeStruct((b, s, d), F32),
        compiler_params=_params(("parallel", "parallel"), 32),
        name="final_rmsnorm",
    )(x, w.reshape(1, d))


def kernel(x, c, ctx, c_ctx, w_ada, b_ada, norm1_w, w_in, rpb, sgu_norm_w, sgu_w, sgu_b,
           grp_norm_na, grp_norm_sgu, w_out, norm2_w, w_ff1, w_ff2, final_norm_w):
    b, s, d = x.shape
    n_ctx = ctx.shape[1]
    d_na = NA_HEADS * HEAD_DIM
    d_sgu = d - d_na
    depth = w_ada.shape[0]
    assert depth == 1

    n_rows = -(-(b + 1) // 8) * 8
    cvecs = jnp.zeros((n_rows, d), F32).at[:b].set(c).at[b].set(c_ctx)
    mod = _adaln(cvecs, w_ada[0], b_ada[0]).reshape(n_rows, N_MOD, d)
    lat = lambda i: mod[:b, i].reshape(b, 1, d)
    cx = lambda i: mod[b, i].reshape(1, 1, d)

    w_in_b = w_in[0].astype(BF16)
    qkv_scale = jnp.concatenate([jnp.full((d_na,), HEAD_DIM ** -0.5 * LOG2E, F32),
                                 jnp.ones((2 * d_na + 2 * d_sgu,), F32)])
    proj = _norm_mod_matmul(x, norm1_w[0], lat(1), lat(0), w_in_b, epilogue="scale_or_gelu",
                            col_scale=qkv_scale, gelu_cols=2 * d_sgu)
    ctx_kv = _norm_mod_matmul(ctx.reshape(1, b * n_ctx, d), norm1_w[0], cx(1), cx(0), w_in_b,
                              n_lo=d_na, n_hi=3 * d_na).reshape(b, n_ctx, 2 * d_na)
    o_na, w_out_b, w_ff1_b, w_ff2_b = _attention(proj, ctx_kv, rpb[0], [w_out[0], w_ff1[0], w_ff2[0]])
    x1 = _mix_out(proj, o_na, x, lat(2), w_out_b, sgu_w[0], sgu_b[0], sgu_norm_w[0],
                  grp_norm_na[0], grp_norm_sgu[0])
    hidden = _norm_mod_matmul(x1, norm2_w[0], lat(4), lat(3), w_ff1_b, epilogue="relu2")
    x2 = _matmul_residual(hidden, w_ff2_b, x1, lat(5))
    return _rmsnorm(x2, final_norm_w)
```

```python
import functools
import math

import jax
import jax.numpy as jnp
from jax import lax
from jax.experimental import pallas as pl
from jax.experimental.pallas import tpu as pltpu

F32 = jnp.float32
BF16 = jnp.bfloat16

GRID_W = 64
NA_HEADS = 16
HEAD_DIM = 128
NA_KH_MAX = 8
NA_KW = 16
SGU_GROUPS = 4
SGU_CHUNK = 128
N_MOD = 6
NORM_EPS = 1e-6
MASK_VALUE = -1e30
LOG2E = math.log2(math.e)

ATT_QROWS = 4
ATT_KROWS = 12
NORM_ROWS = 16
NORM_UNROLL = 4

V7X_VMEM_BYTES = 64 << 20


def _params(semantics, vmem_mib):
    assert (vmem_mib << 20) < V7X_VMEM_BYTES
    return pltpu.CompilerParams(dimension_semantics=semantics, vmem_limit_bytes=vmem_mib << 20)


def _tile(n, pref, align=128, divides=()):
    for t in range(min(pref, n) // align * align, 0, -align):
        if n % t == 0 and all(v % t == 0 for v in divides):
            return t
    assert not any(divides)
    return n


def _adaln_kernel(c_ref, w_ref, b_ref, o_ref):
    c = c_ref[...]
    a = (c * jax.nn.sigmoid(c)).astype(BF16)
    o_ref[...] = jnp.dot(a, w_ref[...].astype(BF16), preferred_element_type=F32) + b_ref[...]


def _adaln(cvecs, w_ada, b_ada):
    r, d = cvecs.shape
    n = w_ada.shape[1]
    tn = _tile(n, 512)
    return pl.pallas_call(
        _adaln_kernel,
        grid=(n // tn,),
        in_specs=[
            pl.BlockSpec((r, d), lambda j: (0, 0)),
            pl.BlockSpec((d, tn), lambda j: (0, j)),
            pl.BlockSpec((1, tn), lambda j: (0, j)),
        ],
        out_specs=pl.BlockSpec((r, tn), lambda j: (0, j)),
        out_shape=jax.ShapeDtypeStruct((r, n), F32),
        compiler_params=_params(("arbitrary",), 40),
        name="adaln",
    )(cvecs, w_ada, b_ada.reshape(1, n))


def _norm_mod_matmul_kernel(x_hbm, nw_ref, sc_ref, sh_ref, w_ref, cs_ref, o_ref, x_buf, h_ref, x_sem, *,
                            rows_chunk, epilogue, gelu_from):
    tm = x_buf.shape[0]
    bb = pl.program_id(0)
    i = pl.program_id(1)
    j = pl.program_id(2)
    n_i = pl.num_programs(1)
    n_tiles = pl.num_programs(0) * n_i

    def x_copy(tile):
        return pltpu.make_async_copy(
            x_hbm.at[tile // n_i, pl.ds((tile % n_i) * tm, tm), :], x_buf, x_sem)

    @pl.when(j == 0)
    def _():
        tile = bb * n_i + i

        @pl.when(tile == 0)
        def _():
            x_copy(tile).start()

        x_copy(tile).wait()

        def body(c, carry):
            r = pl.multiple_of(c * rows_chunk, rows_chunk)
            x = x_buf[pl.ds(r, rows_chunk), :]
            ms = jnp.mean(x * x, axis=-1, keepdims=True)
            gain = nw_ref[...] * (1.0 + sc_ref[...])
            h = x * lax.rsqrt(ms + NORM_EPS) * gain + sh_ref[...]
            h_ref[pl.ds(r, rows_chunk), :] = h.astype(BF16)
            return carry
        lax.fori_loop(0, tm // rows_chunk, body, 0, unroll=NORM_UNROLL)

        @pl.when(tile + 1 < n_tiles)
        def _():
            x_copy(tile + 1).start()

    def matmul():
        return jnp.dot(h_ref[...], w_ref[...], preferred_element_type=F32)

    if epilogue == "plain":
        o_ref[...] = matmul().astype(o_ref.dtype)
    elif epilogue == "relu2":
        o_ref[...] = jnp.square(jnp.maximum(matmul(), 0.0)).astype(o_ref.dtype)
    else:
        assert epilogue == "scale_or_gelu"

        @pl.when(j < gelu_from)
        def _():
            o_ref[...] = (matmul() * cs_ref[...]).astype(o_ref.dtype)

        @pl.when(j >= gelu_from)
        def _():
            o_ref[...] = jax.nn.gelu(matmul()).astype(o_ref.dtype)


def _norm_mod_matmul(x, norm_w, scale, shift, w, *, epilogue="plain", col_scale=None, gelu_cols=0,
                     n_lo=0, n_hi=None, tm=1024, tn=1024, vmem_mib=56):
    b, s, k = x.shape
    n_hi = w.shape[1] if n_hi is None else n_hi
    n = n_hi - n_lo
    tm = _tile(s, tm, align=8)
    tn = _tile(n, tn, divides=(n_lo, n - gelu_cols))
    j0 = n_lo // tn
    if col_scale is None:
        col_scale = jnp.ones((n,), F32)
    per_batch = scale.shape[0] == b and b > 1
    mod_map = (lambda bb, i, j: (bb, 0, 0)) if per_batch else (lambda bb, i, j: (0, 0, 0))
    return pl.pallas_call(
        functools.partial(_norm_mod_matmul_kernel, rows_chunk=min(NORM_ROWS, tm), epilogue=epilogue,
                          gelu_from=(n - gelu_cols) // tn),
        grid=(b, s // tm, n // tn),
        in_specs=[
            pl.BlockSpec(memory_space=pl.ANY),
            pl.BlockSpec((1, k), lambda bb, i, j: (0, 0)),
            pl.BlockSpec((None, 1, k), mod_map),
            pl.BlockSpec((None, 1, k), mod_map),
            pl.BlockSpec((k, tn), lambda bb, i, j: (0, j0 + j)),
            pl.BlockSpec((1, tn), lambda bb, i, j: (0, j)),
        ],
        out_specs=pl.BlockSpec((None, tm, tn), lambda bb, i, j: (bb, i, j)),
        out_shape=jax.ShapeDtypeStruct((b, s, n), BF16),
        scratch_shapes=[pltpu.VMEM((tm, k), F32), pltpu.VMEM((tm, k), BF16),
                        pltpu.SemaphoreType.DMA(())],
        compiler_params=_params(("arbitrary", "arbitrary", "arbitrary"), vmem_mib),
        name="norm_mod_matmul_" + epilogue,
    )(x, norm_w.reshape(1, k), scale, shift, w, col_scale.reshape(1, n))


def _window_structure(rows, rblk):
    kh = min(NA_KH_MAX, rows)
    kstart = min(max(rblk * ATT_QROWS - kh // 2, 0), rows - ATT_KROWS)
    out = []
    for i in range(ATT_QROWS):
        qr = rblk * ATT_QROWS + i
        r0 = min(max(qr - kh // 2, 0), rows - kh)
        row = []
        for j in range(ATT_KROWS):
            kr = kstart + j
            row.append(kr - qr + NA_KH_MAX - 1 if r0 <= kr < r0 + kh else None)
        assert sum(e is not None for e in row) == kh
        out.append(tuple(row))
    return tuple(out)


def _block_types(rows):
    nblk = rows // ATT_QROWS
    reps = [0, min(1, nblk - 1), nblk - 1]
    structs = [_window_structure(rows, r) for r in reps]
    for r in range(nblk):
        t = 0 if r == 0 else (2 if r == nblk - 1 else 1)
        assert _window_structure(rows, r) == structs[t]
    return structs


def _attn_kernel(rpb_ref, q_ref, k_ref, v_ref, kc_ref, vc_ref, *rest, rows, n_cast):
    cast_in, o_ref, cast_out = rest[:n_cast], rest[n_cast], rest[n_cast + 1:2 * n_cast + 1]
    bias_ref, s_bufs, p_bufs, l_bufs = (rest[2 * n_cast + 1], rest[2 * n_cast + 2:2 * n_cast + 4],
                                        rest[2 * n_cast + 4:2 * n_cast + 6], rest[2 * n_cast + 6:])
    for src, dst in zip(cast_in, cast_out):
        dst[...] = src[...].astype(dst.dtype)

    w = GRID_W
    n_ro = 2 * NA_KH_MAX - 1
    n_co = 2 * NA_KW - 1
    nblk = rows // ATT_QROWS
    kh = min(NA_KH_MAX, rows)
    tq = ATT_QROWS * w
    tk = ATT_KROWS * w
    h = pl.program_id(0)
    b = pl.program_id(1)

    @pl.when(b == 0)
    def _():
        qc = lax.broadcasted_iota(jnp.int32, (w, 2 * w), 0)
        lane = lax.broadcasted_iota(jnp.int32, (w, 2 * w), 1)
        kc = lane % w
        cs = jnp.clip(qc - NA_KW // 2, 0, w - NA_KW)
        col_ok = (kc >= cs) & (kc < cs + NA_KW)
        diff = kc - qc + (NA_KW - 1)
        hit = [(diff == d) & col_ok for d in range(n_co)]
        base = h * (n_ro * n_co)
        neg = jnp.full((w, 2 * w), MASK_VALUE, F32)
        tables = []
        for ro in range(n_ro):
            acc = neg
            for d in range(n_co):
                acc = jnp.where(hit[d], rpb_ref[base + ro * n_co + d] * LOG2E, acc)
            tables.append(acc)
        left = lane < w
        for t, struct in enumerate(_block_types(rows)):
            for i in range(ATT_QROWS):
                for jp in range(ATT_KROWS // 2):
                    ro_l, ro_r = struct[i][2 * jp], struct[i][2 * jp + 1]
                    blk_l = neg if ro_l is None else tables[ro_l]
                    blk_r = neg if ro_r is None else tables[ro_r]
                    blk = blk_l if ro_l == ro_r else jnp.where(left, blk_l, blk_r)
                    bias_ref[t, i * w:(i + 1) * w, jp * 2 * w:(jp + 1) * 2 * w] = blk

    nt = (((1,), (1,)), ((), ()))
    last = nblk - 1

    def key_start(r):
        kstart = jnp.clip(r * ATT_QROWS - kh // 2, 0, rows - ATT_KROWS)
        return pl.multiple_of(kstart * w, w)

    def scores(r, slot):
        r = jnp.minimum(r, last)
        q = q_ref[pl.ds(pl.multiple_of(r * tq, tq), tq), :]
        kw = k_ref[pl.ds(key_start(r), tk), :]
        t = jnp.where(r == 0, 0, jnp.where(r == last, 2, 1))
        s_bufs[slot][:, 0:tk] = lax.dot_general(q, kw, nt, preferred_element_type=F32) + bias_ref[t]
        s_bufs[slot][:, tk:] = lax.dot_general(q, kc_ref[...], nt, preferred_element_type=F32)

    def softmax(slot):
        s = s_bufs[slot][...]
        p = jnp.exp2(s - jnp.max(s, axis=-1, keepdims=True))
        l_bufs[slot][...] = 1.0 / jnp.sum(p, axis=-1, keepdims=True)
        p_bufs[slot][...] = p.astype(BF16)

    def values(r, slot):
        vw = v_ref[pl.ds(key_start(r), tk), :]
        o = (jnp.dot(p_bufs[slot][:, 0:tk], vw, preferred_element_type=F32)
             + jnp.dot(p_bufs[slot][:, tk:], vc_ref[...], preferred_element_type=F32))
        o_ref[pl.ds(pl.multiple_of(r * tq, tq), tq), :] = (o * l_bufs[slot][...]).astype(o_ref.dtype)

    scores(0, 0)
    scores(1, 1)
    softmax(0)

    def body(it, carry):
        r = 2 * it
        scores(r + 2, 0)
        softmax(1)
        values(r, 0)
        scores(r + 3, 1)
        softmax(0)
        values(r + 1, 1)
        return carry
    lax.fori_loop(0, nblk // 2, body, 0, unroll=4)


def _attention(proj, ctx_kv, rpb, cast_weights):
    b, s, _ = proj.shape
    c = ctx_kv.shape[1]
    nh, dh = NA_HEADS, HEAD_DIM
    rows = s // GRID_W
    assert s % GRID_W == 0 and rows % (2 * ATT_QROWS) == 0 and rows >= ATT_KROWS
    tq, tk = ATT_QROWS * GRID_W, ATT_KROWS * GRID_W
    steps = nh * b
    cast_specs = []
    for wgt in cast_weights:
        slab = wgt.shape[0] // steps
        assert wgt.shape[0] % steps == 0 and slab % 16 == 0
        cast_specs.append(pl.BlockSpec((slab, wgt.shape[1]), lambda h, bb: (h * b + bb, 0)))
    return pl.pallas_call(
        functools.partial(_attn_kernel, rows=rows, n_cast=len(cast_weights)),
        grid=(nh, b),
        in_specs=[
            pl.BlockSpec(memory_space=pltpu.SMEM),
            pl.BlockSpec((None, s, dh), lambda h, bb: (bb, 0, h)),
            pl.BlockSpec((None, s, dh), lambda h, bb: (bb, 0, nh + h)),
            pl.BlockSpec((None, s, dh), lambda h, bb: (bb, 0, 2 * nh + h)),
            pl.BlockSpec((None, c, dh), lambda h, bb: (bb, 0, h)),
            pl.BlockSpec((None, c, dh), lambda h, bb: (bb, 0, nh + h)),
        ] + cast_specs,
        out_specs=[pl.BlockSpec((None, s, dh), lambda h, bb: (bb, 0, h))] + cast_specs,
        out_shape=[jax.ShapeDtypeStruct((b, s, nh * dh), BF16)]
        + [jax.ShapeDtypeStruct(wgt.shape, BF16) for wgt in cast_weights],
        scratch_shapes=[pltpu.VMEM((3, tq, tk), F32)]
        + [pltpu.VMEM((tq, tk + c), F32)] * 2 + [pltpu.VMEM((tq, tk + c), BF16)] * 2
        + [pltpu.VMEM((tq, 1), F32)] * 2,
        compiler_params=_params(("arbitrary", "arbitrary"), 56),
        name="nbr_attention",
    )(rpb.reshape(-1), proj, proj, proj, ctx_kv, ctx_kv, *cast_weights)


def _mix_out_kernel(proj_hbm, ona_hbm, x_ref, gate_ref, wout_ref, ws_ref, bs_ref, snw_ref,
                    gna_ref, gsg_ref, o_ref, u_ref, g_ref, ona_ref, m_ref, sems, *, u_col):
    tm = u_ref.shape[0]
    d_sgu = u_ref.shape[1]
    d_na = ona_ref.shape[1]
    gd = d_sgu // SGU_GROUPS
    n_i = pl.num_programs(1)
    n_tiles = pl.num_programs(0) * n_i

    def copies(tile):
        bb, rows = tile // n_i, pl.ds((tile % n_i) * tm, tm)
        return (pltpu.make_async_copy(proj_hbm.at[bb, rows, pl.ds(u_col, d_sgu)], u_ref, sems.at[0]),
                pltpu.make_async_copy(proj_hbm.at[bb, rows, pl.ds(u_col + d_sgu, d_sgu)], g_ref,
                                      sems.at[1]),
                pltpu.make_async_copy(ona_hbm.at[bb, rows, :], ona_ref, sems.at[2]))

    @pl.when(pl.program_id(2) == 0)
    def _():
        tile = pl.program_id(0) * n_i + pl.program_id(1)

        @pl.when(tile == 0)
        def _():
            for cp in copies(tile):
                cp.start()

        for cp in copies(tile):
            cp.wait()

        def body(c, carry):
            r = pl.multiple_of(c * SGU_CHUNK, SGU_CHUNK)
            rs = pl.ds(r, SGU_CHUNK)
            gg = g_ref[rs, :].astype(F32)
            ms = jnp.mean(gg * gg, axis=-1, keepdims=True)
            gn = (gg * lax.rsqrt(ms + NORM_EPS) * snw_ref[...]).astype(BF16)
            parts = []
            ssq = jnp.zeros((SGU_CHUNK, 1), F32)
            for g in range(SGU_GROUPS):
                cols = slice(g * gd, (g + 1) * gd)
                mixed = jnp.dot(ws_ref[g], gn[:, cols], preferred_element_type=F32) + bs_ref[g]
                og = u_ref[rs, cols].astype(F32) * mixed
                ssq = ssq + jnp.sum(og * og, axis=-1, keepdims=True)
                parts.append(og)
            inv = lax.rsqrt(ssq / d_sgu + NORM_EPS)
            for g in range(SGU_GROUPS):
                cols = slice(g * gd, (g + 1) * gd)
                m_ref[rs, d_na + g * gd:d_na + (g + 1) * gd] = (
                    parts[g] * inv * gsg_ref[:, cols]).astype(BF16)
            on = ona_ref[rs, :].astype(F32)
            ms = jnp.mean(on * on, axis=-1, keepdims=True)
            m_ref[rs, 0:d_na] = (on * lax.rsqrt(ms + NORM_EPS) * gna_ref[...]).astype(BF16)
            return carry
        lax.fori_loop(0, tm // SGU_CHUNK, body, 0)

        @pl.when(tile + 1 < n_tiles)
        def _():
            for cp in copies(tile + 1):
                cp.start()

    acc = jnp.dot(m_ref[...], wout_ref[...], preferred_element_type=F32)
    o_ref[...] = x_ref[...] + gate_ref[...] * acc


def _mix_out(proj, o_na, x, gate, w_out, sgu_w, sgu_b, sgu_norm_w, gn_na, gn_sgu, *, tm=1024, tn=512):
    b, s, d = x.shape
    d_na = o_na.shape[2]
    d_sgu = d - d_na
    tm = _tile(s, tm)
    tn = _tile(d, tn)
    assert tm % SGU_CHUNK == 0 and proj.shape[2] == 3 * d_na + 2 * d_sgu
    const2 = lambda bb, i, j: (0, 0)
    const3 = lambda bb, i, j: (0, 0, 0)
    return pl.pallas_call(
        functools.partial(_mix_out_kernel, u_col=3 * d_na),
        grid=(b, s // tm, d // tn),
        in_specs=[
            pl.BlockSpec(memory_space=pl.ANY),
            pl.BlockSpec(memory_space=pl.ANY),
            pl.BlockSpec((None, tm, tn), lambda bb, i, j: (bb, i, j)),
            pl.BlockSpec((None, 1, tn), lambda bb, i, j: (bb, 0, j)),
            pl.BlockSpec((d, tn), lambda bb, i, j: (0, j)),
            pl.BlockSpec((SGU_GROUPS, SGU_CHUNK, SGU_CHUNK), const3),
            pl.BlockSpec((SGU_GROUPS, SGU_CHUNK, 1), const3),
            pl.BlockSpec((1, d_sgu), const2),
            pl.BlockSpec((1, d_na), const2),
            pl.BlockSpec((1, d_sgu), const2),
        ],
        out_specs=pl.BlockSpec((None, tm, tn), lambda bb, i, j: (bb, i, j)),
        out_shape=jax.ShapeDtypeStruct((b, s, d), F32),
        scratch_shapes=[pltpu.VMEM((tm, d_sgu), BF16), pltpu.VMEM((tm, d_sgu), BF16),
                        pltpu.VMEM((tm, d_na), BF16), pltpu.VMEM((tm, d), BF16),
                        pltpu.SemaphoreType.DMA((3,))],
        compiler_params=_params(("arbitrary", "arbitrary", "arbitrary"), 48),
        name="mix_out",
    )(proj, o_na, x, gate, w_out, sgu_w.astype(BF16),
      sgu_b.reshape(SGU_GROUPS, SGU_CHUNK, 1), sgu_norm_w.reshape(1, d_sgu),
      gn_na.reshape(1, d_na), gn_sgu.reshape(1, d_sgu))


def _matmul_residual_kernel(a_ref, w_ref, x_ref, g_ref, o_ref, acc_ref):
    kk = pl.program_id(3)
    last = pl.num_programs(3) - 1

    def partial_product():
        return jnp.dot(a_ref[...], w_ref[...], preferred_element_type=F32)

    @pl.when(kk == 0)
    def _():
        acc_ref[...] = partial_product()

    @pl.when((kk > 0) & (kk < last))
    def _():
        acc_ref[...] += partial_product()

    @pl.when(kk == last)
    def _():
        o_ref[...] = x_ref[...] + g_ref[...] * (acc_ref[...] + partial_product())


def _matmul_residual(a, w, x, gate, *, tm=1024, tn=1024, tk=2048):
    b, s, k = a.shape
    n = w.shape[1]
    tm, tn, tk = _tile(s, tm), _tile(n, tn), _tile(k, tk)
    assert k // tk >= 2
    return pl.pallas_call(
        _matmul_residual_kernel,
        grid=(b, s // tm, n // tn, k // tk),
        in_specs=[
            pl.BlockSpec((None, tm, tk), lambda bb, i, j, kk: (bb, i, kk)),
            pl.BlockSpec((tk, tn), lambda bb, i, j, kk: (kk, j)),
            pl.BlockSpec((None, tm, tn), lambda bb, i, j, kk: (bb, i, j)),
            pl.BlockSpec((None, 1, tn), lambda bb, i, j, kk: (bb, 0, j)),
        ],
        out_specs=pl.BlockSpec((None, tm, tn), lambda bb, i, j, kk: (bb, i, j)),
        out_shape=jax.ShapeDtypeStruct((b, s, n), F32),
        scratch_shapes=[pltpu.VMEM((tm, tn), F32)],
        compiler_params=_params(("parallel", "parallel", "parallel", "arbitrary"), 48),
        name="ff2_residual",
    )(a, w, x, gate)


def _rmsnorm_kernel(x_ref, w_ref, o_ref):
    x = x_ref[...]
    ms = jnp.mean(x * x, axis=-1, keepdims=True)
    o_ref[...] = x * lax.rsqrt(ms + NORM_EPS) * w_ref[...]


def _rmsnorm(x, w, *, tm=256):
    b, s, d = x.shape
    tm = _tile(s, tm)
    return pl.pallas_call(
        _rmsnorm_kernel,
        grid=(b, s // tm),
        in_specs=[pl.BlockSpec((None, tm, d), lambda bb, i: (bb, i, 0)),
                  pl.BlockSpec((1, d), lambda bb, i: (0, 0))],
        out_specs=pl.BlockSpec((None, tm, d), lambda bb, i: (bb, i, 0)),
        out_shape=jax.ShapeDtypeStruct((b, s, d), F32),
        compiler_params=_params(("parallel", "parallel"), 32),
        name="final_rmsnorm",
    )(x, w.reshape(1, d))


def kernel(x, c, ctx, c_ctx, w_ada, b_ada, norm1_w, w_in, rpb, sgu_norm_w, sgu_w, sgu_b,
           grp_norm_na, grp_norm_sgu, w_out, norm2_w, w_ff1, w_ff2, final_norm_w):
    b, s, d = x.shape
    n_ctx = ctx.shape[1]
    d_na = NA_HEADS * HEAD_DIM
    d_sgu = d - d_na
    depth = w_ada.shape[0]
    assert depth == 1

    n_rows = -(-(b + 1) // 8) * 8
    cvecs = jnp.zeros((n_rows, d), F32).at[:b].set(c).at[b].set(c_ctx)
    mod = _adaln(cvecs, w_ada[0], b_ada[0]).reshape(n_rows, N_MOD, d)
    lat = lambda i: mod[:b, i].reshape(b, 1, d)
    cx = lambda i: mod[b, i].reshape(1, 1, d)

    w_in_b = w_in[0].astype(BF16)
    qkv_scale = jnp.concatenate([jnp.full((d_na,), HEAD_DIM ** -0.5 * LOG2E, F32),
                                 jnp.ones((2 * d_na + 2 * d_sgu,), F32)])
    proj = _norm_mod_matmul(x, norm1_w[0], lat(1), lat(0), w_in_b, epilogue="scale_or_gelu",
                            col_scale=qkv_scale, gelu_cols=2 * d_sgu)
    ctx_kv = _norm_mod_matmul(ctx.reshape(1, b * n_ctx, d), norm1_w[0], cx(1), cx(0), w_in_b,
                              n_lo=d_na, n_hi=3 * d_na).reshape(b, n_ctx, 2 * d_na)
    o_na, w_out_b, w_ff1_b, w_ff2_b = _attention(proj, ctx_kv, rpb[0], [w_out[0], w_ff1[0], w_ff2[0]])
    x1 = _mix_out(proj, o_na, x, lat(2), w_out_b, sgu_w[0], sgu_b[0], sgu_norm_w[0],
                  grp_norm_na[0], grp_norm_sgu[0])
    hidden = _norm_mod_matmul(x1, norm2_w[0], lat(4), lat(3), w_ff1_b, epilogue="relu2")
    x2 = _matmul_residual(hidden, w_ff2_b, x1, lat(5))
    return _rmsnorm(x2, final_norm_w)
```

```python
import functools
import math

import jax
import jax.numpy as jnp
from jax import lax
from jax.experimental import pallas as pl
from jax.experimental.pallas import tpu as pltpu

F32 = jnp.float32
BF16 = jnp.bfloat16

GRID_W = 64
NA_HEADS = 16
HEAD_DIM = 128
NA_KH_MAX = 8
NA_KW = 16
SGU_GROUPS = 4
SGU_CHUNK = 128
N_MOD = 6
NORM_EPS = 1e-6
MASK_VALUE = -1e30
LOG2E = math.log2(math.e)

ATT_QROWS = 4
ATT_KROWS = 12
NORM_ROWS = 16
NORM_UNROLL = 4

V7X_VMEM_BYTES = 64 << 20


def _params(semantics, vmem_mib):
    assert (vmem_mib << 20) < V7X_VMEM_BYTES
    return pltpu.CompilerParams(dimension_semantics=semantics, vmem_limit_bytes=vmem_mib << 20)


def _tile(n, pref, align=128, divides=()):
    for t in range(min(pref, n) // align * align, 0, -align):
        if n % t == 0 and all(v % t == 0 for v in divides):
            return t
    assert not any(divides)
    return n


def _cast_specs(weights, steps, step_index_map):
    specs = []
    for wgt in weights:
        slab = wgt.shape[0] // steps
        assert wgt.shape[0] % steps == 0 and slab % 16 == 0
        specs.append(pl.BlockSpec((slab, wgt.shape[1]), step_index_map))
    return specs


def _adaln_kernel(c_ref, w_ref, b_ref, o_ref):
    c = c_ref[...]
    a = (c * jax.nn.sigmoid(c)).astype(BF16)
    o_ref[...] = jnp.dot(a, w_ref[...].astype(BF16), preferred_element_type=F32) + b_ref[...]


def _adaln(cvecs, w_ada, b_ada):
    r, d = cvecs.shape
    n = w_ada.shape[1]
    tn = _tile(n, 512)
    return pl.pallas_call(
        _adaln_kernel,
        grid=(n // tn,),
        in_specs=[
            pl.BlockSpec((r, d), lambda j: (0, 0)),
            pl.BlockSpec((d, tn), lambda j: (0, j)),
            pl.BlockSpec((1, tn), lambda j: (0, j)),
        ],
        out_specs=pl.BlockSpec((r, tn), lambda j: (0, j)),
        out_shape=jax.ShapeDtypeStruct((r, n), F32),
        compiler_params=_params(("arbitrary",), 40),
        name="adaln",
    )(cvecs, w_ada, b_ada.reshape(1, n))


def _norm_mod_matmul_kernel(x_hbm, nw_ref, sc_ref, sh_ref, w_ref, cs_ref, *rest,
                            rows_chunk, epilogue, gelu_from, n_cast):
    cast_in, o_ref, cast_out = rest[:n_cast], rest[n_cast], rest[n_cast + 1:2 * n_cast + 1]
    x_buf, h_ref, x_sem = rest[2 * n_cast + 1:]
    for src, dst in zip(cast_in, cast_out):
        dst[...] = src[...].astype(dst.dtype)

    tm = x_buf.shape[0]
    bb = pl.program_id(0)
    i = pl.program_id(1)
    j = pl.program_id(2)
    n_i = pl.num_programs(1)
    n_tiles = pl.num_programs(0) * n_i

    def x_copy(tile):
        return pltpu.make_async_copy(
            x_hbm.at[tile // n_i, pl.ds((tile % n_i) * tm, tm), :], x_buf, x_sem)

    @pl.when(j == 0)
    def _():
        tile = bb * n_i + i

        @pl.when(tile == 0)
        def _():
            x_copy(tile).start()

        x_copy(tile).wait()

        def body(c, carry):
            r = pl.multiple_of(c * rows_chunk, rows_chunk)
            x = x_buf[pl.ds(r, rows_chunk), :]
            ms = jnp.mean(x * x, axis=-1, keepdims=True)
            gain = nw_ref[...] * (1.0 + sc_ref[...])
            h = x * lax.rsqrt(ms + NORM_EPS) * gain + sh_ref[...]
            h_ref[pl.ds(r, rows_chunk), :] = h.astype(BF16)
            return carry
        lax.fori_loop(0, tm // rows_chunk, body, 0, unroll=NORM_UNROLL)

        @pl.when(tile + 1 < n_tiles)
        def _():
            x_copy(tile + 1).start()

    def matmul():
        return jnp.dot(h_ref[...], w_ref[...], preferred_element_type=F32)

    if epilogue == "plain":
        o_ref[...] = matmul().astype(o_ref.dtype)
    elif epilogue == "relu2":
        o_ref[...] = jnp.square(jnp.maximum(matmul(), 0.0)).astype(o_ref.dtype)
    else:
        assert epilogue == "scale_or_gelu"

        @pl.when(j < gelu_from)
        def _():
            o_ref[...] = (matmul() * cs_ref[...]).astype(o_ref.dtype)

        @pl.when(j >= gelu_from)
        def _():
            o_ref[...] = jax.nn.gelu(matmul()).astype(o_ref.dtype)


def _norm_mod_matmul(x, norm_w, scale, shift, w, *, epilogue="plain", col_scale=None, gelu_cols=0,
                     n_lo=0, n_hi=None, cast_weights=(), tm=1024, tn=1024, vmem_mib=56):
    b, s, k = x.shape
    n_hi = w.shape[1] if n_hi is None else n_hi
    n = n_hi - n_lo
    tm = _tile(s, tm, align=8)
    tn = _tile(n, tn, divides=(n_lo, n - gelu_cols))
    j0 = n_lo // tn
    n_i, n_j = s // tm, n // tn
    if col_scale is None:
        col_scale = jnp.ones((n,), F32)
    per_batch = scale.shape[0] == b and b > 1
    mod_map = (lambda bb, i, j: (bb, 0, 0)) if per_batch else (lambda bb, i, j: (0, 0, 0))
    cast_specs = _cast_specs(cast_weights, b * n_i * n_j, lambda bb, i, j: ((bb * n_i + i) * n_j + j, 0))
    out = pl.pallas_call(
        functools.partial(_norm_mod_matmul_kernel, rows_chunk=min(NORM_ROWS, tm), epilogue=epilogue,
                          gelu_from=(n - gelu_cols) // tn, n_cast=len(cast_weights)),
        grid=(b, n_i, n_j),
        in_specs=[
            pl.BlockSpec(memory_space=pl.ANY),
            pl.BlockSpec((1, k), lambda bb, i, j: (0, 0)),
            pl.BlockSpec((None, 1, k), mod_map),
            pl.BlockSpec((None, 1, k), mod_map),
            pl.BlockSpec((k, tn), lambda bb, i, j: (0, j0 + j)),
            pl.BlockSpec((1, tn), lambda bb, i, j: (0, j)),
        ] + cast_specs,
        out_specs=[pl.BlockSpec((None, tm, tn), lambda bb, i, j: (bb, i, j))] + cast_specs,
        out_shape=[jax.ShapeDtypeStruct((b, s, n), BF16)]
        + [jax.ShapeDtypeStruct(wgt.shape, BF16) for wgt in cast_weights],
        scratch_shapes=[pltpu.VMEM((tm, k), F32), pltpu.VMEM((tm, k), BF16),
                        pltpu.SemaphoreType.DMA(())],
        compiler_params=_params(("arbitrary", "arbitrary", "arbitrary"), vmem_mib),
        name="norm_mod_matmul_" + epilogue,
    )(x, norm_w.reshape(1, k), scale, shift, w, col_scale.reshape(1, n), *cast_weights)
    return out if cast_weights else out[0]


def _window_structure(rows, rblk):
    kh = min(NA_KH_MAX, rows)
    kstart = min(max(rblk * ATT_QROWS - kh // 2, 0), rows - ATT_KROWS)
    out = []
    for i in range(ATT_QROWS):
        qr = rblk * ATT_QROWS + i
        r0 = min(max(qr - kh // 2, 0), rows - kh)
        row = []
        for j in range(ATT_KROWS):
            kr = kstart + j
            row.append(kr - qr + NA_KH_MAX - 1 if r0 <= kr < r0 + kh else None)
        assert sum(e is not None for e in row) == kh
        out.append(tuple(row))
    return tuple(out)


def _block_types(rows):
    nblk = rows // ATT_QROWS
    reps = [0, min(1, nblk - 1), nblk - 1]
    structs = [_window_structure(rows, r) for r in reps]
    for r in range(nblk):
        t = 0 if r == 0 else (2 if r == nblk - 1 else 1)
        assert _window_structure(rows, r) == structs[t]
    return structs


def _attn_kernel(rpb_ref, q_ref, k_ref, v_ref, kc_ref, vc_ref, *rest, rows, n_cast):
    cast_in, o_ref, cast_out = rest[:n_cast], rest[n_cast], rest[n_cast + 1:2 * n_cast + 1]
    bias_ref, s_bufs, p_bufs, l_bufs = (rest[2 * n_cast + 1], rest[2 * n_cast + 2:2 * n_cast + 4],
                                        rest[2 * n_cast + 4:2 * n_cast + 6], rest[2 * n_cast + 6:])
    for src, dst in zip(cast_in, cast_out):
        dst[...] = src[...].astype(dst.dtype)

    w = GRID_W
    n_ro = 2 * NA_KH_MAX - 1
    n_co = 2 * NA_KW - 1
    nblk = rows // ATT_QROWS
    kh = min(NA_KH_MAX, rows)
    tq = ATT_QROWS * w
    tk = ATT_KROWS * w
    h = pl.program_id(0)
    b = pl.program_id(1)

    @pl.when(b == 0)
    def _():
        qc = lax.broadcasted_iota(jnp.int32, (w, 2 * w), 0)
        lane = lax.broadcasted_iota(jnp.int32, (w, 2 * w), 1)
        kc = lane % w
        cs = jnp.clip(qc - NA_KW // 2, 0, w - NA_KW)
        col_ok = (kc >= cs) & (kc < cs + NA_KW)
        diff = kc - qc + (NA_KW - 1)
        hit = [(diff == d) & col_ok for d in range(n_co)]
        base = h * (n_ro * n_co)
        neg = jnp.full((w, 2 * w), MASK_VALUE, F32)
        tables = []
        for ro in range(n_ro):
            acc = neg
            for d in range(n_co):
                acc = jnp.where(hit[d], rpb_ref[base + ro * n_co + d] * LOG2E, acc)
            tables.append(acc)
        left = lane < w
        for t, struct in enumerate(_block_types(rows)):
            for i in range(ATT_QROWS):
                for jp in range(ATT_KROWS // 2):
                    ro_l, ro_r = struct[i][2 * jp], struct[i][2 * jp + 1]
                    blk_l = neg if ro_l is None else tables[ro_l]
                    blk_r = neg if ro_r is None else tables[ro_r]
                    blk = blk_l if ro_l == ro_r else jnp.where(left, blk_l, blk_r)
                    bias_ref[t, i * w:(i + 1) * w, jp * 2 * w:(jp + 1) * 2 * w] = blk

    nt = (((1,), (1,)), ((), ()))
    last = nblk - 1

    def key_start(r):
        kstart = jnp.clip(r * ATT_QROWS - kh // 2, 0, rows - ATT_KROWS)
        return pl.multiple_of(kstart * w, w)

    def scores(r, slot):
        r = jnp.minimum(r, last)
        q = q_ref[pl.ds(pl.multiple_of(r * tq, tq), tq), :]
        kw = k_ref[pl.ds(key_start(r), tk), :]
        t = jnp.where(r == 0, 0, jnp.where(r == last, 2, 1))
        s_bufs[slot][:, 0:tk] = lax.dot_general(q, kw, nt, preferred_element_type=F32) + bias_ref[t]
        s_bufs[slot][:, tk:] = lax.dot_general(q, kc_ref[...], nt, preferred_element_type=F32)

    def softmax(slot):
        s = s_bufs[slot][...]
        p = jnp.exp2(s - jnp.max(s, axis=-1, keepdims=True))
        l_bufs[slot][...] = 1.0 / jnp.sum(p, axis=-1, keepdims=True)
        p_bufs[slot][...] = p.astype(BF16)

    def values(r, slot):
        vw = v_ref[pl.ds(key_start(r), tk), :]
        o = (jnp.dot(p_bufs[slot][:, 0:tk], vw, preferred_element_type=F32)
             + jnp.dot(p_bufs[slot][:, tk:], vc_ref[...], preferred_element_type=F32))
        o_ref[pl.ds(pl.multiple_of(r * tq, tq), tq), :] = (o * l_bufs[slot][...]).astype(o_ref.dtype)

    scores(0, 0)
    scores(1, 1)
    softmax(0)

    def body(it, carry):
        r = 2 * it
        scores(r + 2, 0)
        softmax(1)
        values(r, 0)
        scores(r + 3, 1)
        softmax(0)
        values(r + 1, 1)
        return carry
    lax.fori_loop(0, nblk // 2, body, 0, unroll=4)


def _attention(proj, ctx_kv, rpb, cast_weights):
    b, s, _ = proj.shape
    c = ctx_kv.shape[1]
    nh, dh = NA_HEADS, HEAD_DIM
    rows = s // GRID_W
    assert s % GRID_W == 0 and rows % (2 * ATT_QROWS) == 0 and rows >= ATT_KROWS
    tq, tk = ATT_QROWS * GRID_W, ATT_KROWS * GRID_W
    cast_specs = _cast_specs(cast_weights, nh * b, lambda h, bb: (h * b + bb, 0))
    return pl.pallas_call(
        functools.partial(_attn_kernel, rows=rows, n_cast=len(cast_weights)),
        grid=(nh, b),
        in_specs=[
            pl.BlockSpec(memory_space=pltpu.SMEM),
            pl.BlockSpec((None, s, dh), lambda h, bb: (bb, 0, h)),
            pl.BlockSpec((None, s, dh), lambda h, bb: (bb, 0, nh + h)),
            pl.BlockSpec((None, s, dh), lambda h, bb: (bb, 0, 2 * nh + h)),
            pl.BlockSpec((None, c, dh), lambda h, bb: (bb, 0, h)),
            pl.BlockSpec((None, c, dh), lambda h, bb: (bb, 0, nh + h)),
        ] + cast_specs,
        out_specs=[pl.BlockSpec((None, s, dh), lambda h, bb: (bb, 0, h))] + cast_specs,
        out_shape=[jax.ShapeDtypeStruct((b, s, nh * dh), BF16)]
        + [jax.ShapeDtypeStruct(wgt.shape, BF16) for wgt in cast_weights],
        scratch_shapes=[pltpu.VMEM((3, tq, tk), F32)]
        + [pltpu.VMEM((tq, tk + c), F32)] * 2 + [pltpu.VMEM((tq, tk + c), BF16)] * 2
        + [pltpu.VMEM((tq, 1), F32)] * 2,
        compiler_params=_params(("arbitrary", "arbitrary"), 56),
        name="nbr_attention",
    )(rpb.reshape(-1), proj, proj, proj, ctx_kv, ctx_kv, *cast_weights)


def _mix_out_kernel(proj_hbm, ona_hbm, x_ref, gate_ref, wout_ref, ws_ref, bs_ref, snw_ref,
                    gna_ref, gsg_ref, o_ref, u_ref, g_ref, ona_ref, m_ref, sems, *, u_col):
    tm = u_ref.shape[0]
    d_sgu = u_ref.shape[1]
    d_na = ona_ref.shape[1]
    gd = d_sgu // SGU_GROUPS
    n_i = pl.num_programs(1)
    n_tiles = pl.num_programs(0) * n_i

    def copies(tile):
        bb, rows = tile // n_i, pl.ds((tile % n_i) * tm, tm)
        return (pltpu.make_async_copy(proj_hbm.at[bb, rows, pl.ds(u_col, d_sgu)], u_ref, sems.at[0]),
                pltpu.make_async_copy(proj_hbm.at[bb, rows, pl.ds(u_col + d_sgu, d_sgu)], g_ref,
                                      sems.at[1]),
                pltpu.make_async_copy(ona_hbm.at[bb, rows, :], ona_ref, sems.at[2]))

    @pl.when(pl.program_id(2) == 0)
    def _():
        tile = pl.program_id(0) * n_i + pl.program_id(1)

        @pl.when(tile == 0)
        def _():
            for cp in copies(tile):
                cp.start()

        for cp in copies(tile):
            cp.wait()

        def body(c, carry):
            r = pl.multiple_of(c * SGU_CHUNK, SGU_CHUNK)
            rs = pl.ds(r, SGU_CHUNK)
            gg = g_ref[rs, :].astype(F32)
            ms = jnp.mean(gg * gg, axis=-1, keepdims=True)
            gn = (gg * lax.rsqrt(ms + NORM_EPS) * snw_ref[...]).astype(BF16)
            parts = []
            ssq = jnp.zeros((SGU_CHUNK, 1), F32)
            for g in range(SGU_GROUPS):
                cols = slice(g * gd, (g + 1) * gd)
                mixed = jnp.dot(ws_ref[g], gn[:, cols], preferred_element_type=F32) + bs_ref[g]
                og = u_ref[rs, cols].astype(F32) * mixed
                ssq = ssq + jnp.sum(og * og, axis=-1, keepdims=True)
                parts.append(og)
            inv = lax.rsqrt(ssq / d_sgu + NORM_EPS)
            for g in range(SGU_GROUPS):
                cols = slice(g * gd, (g + 1) * gd)
                m_ref[rs, d_na + g * gd:d_na + (g + 1) * gd] = (
                    parts[g] * inv * gsg_ref[:, cols]).astype(BF16)
            on = ona_ref[rs, :].astype(F32)
            ms = jnp.mean(on * on, axis=-1, keepdims=True)
            m_ref[rs, 0:d_na] = (on * lax.rsqrt(ms + NORM_EPS) * gna_ref[...]).astype(BF16)
            return carry
        lax.fori_loop(0, tm // SGU_CHUNK, body, 0)

        @pl.when(tile + 1 < n_tiles)
        def _():
            for cp in copies(tile + 1):
                cp.start()

    acc = jnp.dot(m_ref[...], wout_ref[...], preferred_element_type=F32)
    o_ref[...] = x_ref[...] + gate_ref[...] * acc


def _mix_out(proj, o_na, x, gate, w_out, sgu_w, sgu_b, sgu_norm_w, gn_na, gn_sgu, *, tm=1024, tn=512):
    b, s, d = x.shape
    d_na = o_na.shape[2]
    d_sgu = d - d_na
    tm = _tile(s, tm)
    tn = _tile(d, tn)
    assert tm % SGU_CHUNK == 0 and proj.shape[2] == 3 * d_na + 2 * d_sgu
    const2 = lambda bb, i, j: (0, 0)
    const3 = lambda bb, i, j: (0, 0, 0)
    return pl.pallas_call(
        functools.partial(_mix_out_kernel, u_col=3 * d_na),
        grid=(b, s // tm, d // tn),
        in_specs=[
            pl.BlockSpec(memory_space=pl.ANY),
            pl.BlockSpec(memory_space=pl.ANY),
            pl.BlockSpec((None, tm, tn), lambda bb, i, j: (bb, i, j)),
            pl.BlockSpec((None, 1, tn), lambda bb, i, j: (bb, 0, j)),
            pl.BlockSpec((d, tn), lambda bb, i, j: (0, j)),
            pl.BlockSpec((SGU_GROUPS, SGU_CHUNK, SGU_CHUNK), const3),
            pl.BlockSpec((SGU_GROUPS, SGU_CHUNK, 1), const3),
            pl.BlockSpec((1, d_sgu), const2),
            pl.BlockSpec((1, d_na), const2),
            pl.BlockSpec((1, d_sgu), const2),
        ],
        out_specs=pl.BlockSpec((None, tm, tn), lambda bb, i, j: (bb, i, j)),
        out_shape=jax.ShapeDtypeStruct((b, s, d), F32),
        scratch_shapes=[pltpu.VMEM((tm, d_sgu), BF16), pltpu.VMEM((tm, d_sgu), BF16),
                        pltpu.VMEM((tm, d_na), BF16), pltpu.VMEM((tm, d), BF16),
                        pltpu.SemaphoreType.DMA((3,))],
        compiler_params=_params(("arbitrary", "arbitrary", "arbitrary"), 48),
        name="mix_out",
    )(proj, o_na, x, gate, w_out, sgu_w.astype(BF16),
      sgu_b.reshape(SGU_GROUPS, SGU_CHUNK, 1), sgu_norm_w.reshape(1, d_sgu),
      gn_na.reshape(1, d_na), gn_sgu.reshape(1, d_sgu))


def _matmul_residual_kernel(a_ref, w_ref, x_ref, g_ref, o_ref, acc_ref):
    kk = pl.program_id(2)
    j = pl.program_id(3)
    last = pl.num_programs(2) - 1

    def partial_product():
        return jnp.dot(a_ref[...], w_ref[...], preferred_element_type=F32)

    @pl.when(kk == 0)
    def _():
        acc_ref[j] = partial_product()

    @pl.when((kk > 0) & (kk < last))
    def _():
        acc_ref[j] += partial_product()

    @pl.when(kk == last)
    def _():
        o_ref[...] = x_ref[...] + g_ref[...] * (acc_ref[j] + partial_product())


def _matmul_residual(a, w, x, gate, *, tm=1024, tn=1024, tk=2048):
    b, s, k = a.shape
    n = w.shape[1]
    tm, tn, tk = _tile(s, tm), _tile(n, tn), _tile(k, tk)
    n_k = k // tk
    assert n_k >= 2
    io_map = lambda bb, i, kk, j: (bb, i, jnp.where(kk == n_k - 1, j, 0))
    return pl.pallas_call(
        _matmul_residual_kernel,
        grid=(b, s // tm, n_k, n // tn),
        in_specs=[
            pl.BlockSpec((None, tm, tk), lambda bb, i, kk, j: (bb, i, kk)),
            pl.BlockSpec((tk, tn), lambda bb, i, kk, j: (kk, j)),
            pl.BlockSpec((None, tm, tn), io_map),
            pl.BlockSpec((None, 1, tn), lambda bb, i, kk, j: (bb, 0, j)),
        ],
        out_specs=pl.BlockSpec((None, tm, tn), io_map),
        out_shape=jax.ShapeDtypeStruct((b, s, n), F32),
        scratch_shapes=[pltpu.VMEM((n // tn, tm, tn), F32)],
        compiler_params=_params(("arbitrary", "arbitrary", "arbitrary", "arbitrary"), 56),
        name="ff2_residual",
    )(a, w, x, gate)


def _rmsnorm_kernel(x_ref, w_ref, o_ref):
    x = x_ref[...]
    ms = jnp.mean(x * x, axis=-1, keepdims=True)
    o_ref[...] = x * lax.rsqrt(ms + NORM_EPS) * w_ref[...]


def _rmsnorm(x, w, *, tm=256):
    b, s, d = x.shape
    tm = _tile(s, tm)
    return pl.pallas_call(
        _rmsnorm_kernel,
        grid=(b, s // tm),
        in_specs=[pl.BlockSpec((None, tm, d), lambda bb, i: (bb, i, 0)),
                  pl.BlockSpec((1, d), lambda bb, i: (0, 0))],
        out_specs=pl.BlockSpec((None, tm, d), lambda bb, i: (bb, i, 0)),
        out_shape=jax.ShapeDtypeStruct((b, s, d), F32),
        compiler_params=_params(("parallel", "parallel"), 32),
        name="final_rmsnorm",
    )(x, w.reshape(1, d))


def kernel(x, c, ctx, c_ctx, w_ada, b_ada, norm1_w, w_in, rpb, sgu_norm_w, sgu_w, sgu_b,
           grp_norm_na, grp_norm_sgu, w_out, norm2_w, w_ff1, w_ff2, final_norm_w):
    b, s, d = x.shape
    n_ctx = ctx.shape[1]
    d_na = NA_HEADS * HEAD_DIM
    d_sgu = d - d_na
    depth = w_ada.shape[0]
    assert depth == 1

    n_rows = -(-(b + 1) // 8) * 8
    cvecs = jnp.zeros((n_rows, d), F32).at[:b].set(c).at[b].set(c_ctx)
    mod = _adaln(cvecs, w_ada[0], b_ada[0]).reshape(n_rows, N_MOD, d)
    lat = lambda i: mod[:b, i].reshape(b, 1, d)
    cx = lambda i: mod[b, i].reshape(1, 1, d)

    w_in_b = w_in[0].astype(BF16)
    qkv_scale = jnp.concatenate([jnp.full((d_na,), HEAD_DIM ** -0.5 * LOG2E, F32),
                                 jnp.ones((2 * d_na + 2 * d_sgu,), F32)])
    proj = _norm_mod_matmul(x, norm1_w[0], lat(1), lat(0), w_in_b, epilogue="scale_or_gelu",
                            col_scale=qkv_scale, gelu_cols=2 * d_sgu)
    ctx_kv = _norm_mod_matmul(ctx.reshape(1, b * n_ctx, d), norm1_w[0], cx(1), cx(0), w_in_b,
                              n_lo=d_na, n_hi=3 * d_na).reshape(b, n_ctx, 2 * d_na)
    o_na, w_out_b, w_ff1_b = _attention(proj, ctx_kv, rpb[0], [w_out[0], w_ff1[0]])
    x1 = _mix_out(proj, o_na, x, lat(2), w_out_b, sgu_w[0], sgu_b[0], sgu_norm_w[0],
                  grp_norm_na[0], grp_norm_sgu[0])
    hidden, w_ff2_b = _norm_mod_matmul(x1, norm2_w[0], lat(4), lat(3), w_ff1_b, epilogue="relu2",
                                       cast_weights=[w_ff2[0]])
    x2 = _matmul_residual(hidden, w_ff2_b, x1, lat(5))
    return _rmsnorm(x2, final_norm_w)
```

```python
import functools
import math

import jax
import jax.numpy as jnp
from jax import lax
from jax.experimental import pallas as pl
from jax.experimental.pallas import tpu as pltpu

F32 = jnp.float32
BF16 = jnp.bfloat16

GRID_W = 64
NA_HEADS = 16
HEAD_DIM = 128
NA_KH_MAX = 8
NA_KW = 16
SGU_GROUPS = 4
SGU_CHUNK = 128
N_MOD = 6
NORM_EPS = 1e-6
MASK_VALUE = -1e30
LOG2E = math.log2(math.e)

ATT_QROWS = 4
ATT_KROWS = 12
NORM_ROWS = 16
NORM_UNROLL = 4

V7X_VMEM_BYTES = 64 << 20


def _params(semantics, vmem_mib):
    assert (vmem_mib << 20) < V7X_VMEM_BYTES
    return pltpu.CompilerParams(dimension_semantics=semantics, vmem_limit_bytes=vmem_mib << 20)


def _tile(n, pref, align=128, divides=()):
    for t in range(min(pref, n) // align * align, 0, -align):
        if n % t == 0 and all(v % t == 0 for v in divides):
            return t
    assert not any(divides)
    return n


def _cast_specs(weights, steps, step_index_map):
    specs = []
    for wgt in weights:
        slab = wgt.shape[0] // steps
        assert wgt.shape[0] % steps == 0 and slab % 16 == 0
        specs.append(pl.BlockSpec((slab, wgt.shape[1]), step_index_map))
    return specs


def _adaln_kernel(c_ref, w_ref, b_ref, o_ref):
    c = c_ref[...]
    a = (c * jax.nn.sigmoid(c)).astype(BF16)
    o_ref[...] = jnp.dot(a, w_ref[...].astype(BF16), preferred_element_type=F32) + b_ref[...]


def _adaln(cvecs, w_ada, b_ada):
    r, d = cvecs.shape
    n = w_ada.shape[1]
    tn = _tile(n, 512)
    return pl.pallas_call(
        _adaln_kernel,
        grid=(n // tn,),
        in_specs=[
            pl.BlockSpec((r, d), lambda j: (0, 0)),
            pl.BlockSpec((d, tn), lambda j: (0, j)),
            pl.BlockSpec((1, tn), lambda j: (0, j)),
        ],
        out_specs=pl.BlockSpec((r, tn), lambda j: (0, j)),
        out_shape=jax.ShapeDtypeStruct((r, n), F32),
        compiler_params=_params(("arbitrary",), 40),
        name="adaln",
    )(cvecs, w_ada, b_ada.reshape(1, n))


def _gelu_tanh(x):
    a = -2.0 * math.sqrt(2.0 / math.pi) * LOG2E
    return x / (1.0 + jnp.exp2(x * (a + (a * 0.044715) * (x * x))))


def _norm_mod_matmul_kernel(x_hbm, nw_ref, sc_ref, sh_ref, w_ref, cs_ref, *rest,
                            rows_chunk, epilogue, gelu_from):
    outs, (x_buf, h_ref, x_sem) = rest[:-3], rest[-3:]
    tm = x_buf.shape[0]
    bb = pl.program_id(0)
    i = pl.program_id(1)
    j = pl.program_id(2)
    n_i = pl.num_programs(1)
    n_tiles = pl.num_programs(0) * n_i

    def x_copy(tile):
        return pltpu.make_async_copy(
            x_hbm.at[tile // n_i, pl.ds((tile % n_i) * tm, tm), :], x_buf, x_sem)

    @pl.when(j == 0)
    def _():
        tile = bb * n_i + i

        @pl.when(tile == 0)
        def _():
            x_copy(tile).start()

        x_copy(tile).wait()

        def body(c, carry):
            r = pl.multiple_of(c * rows_chunk, rows_chunk)
            x = x_buf[pl.ds(r, rows_chunk), :]
            ms = jnp.mean(x * x, axis=-1, keepdims=True)
            gain = nw_ref[...] * (1.0 + sc_ref[...])
            h = x * lax.rsqrt(ms + NORM_EPS) * gain + sh_ref[...]
            h_ref[pl.ds(r, rows_chunk), :] = h.astype(BF16)
            return carry
        lax.fori_loop(0, tm // rows_chunk, body, 0, unroll=NORM_UNROLL)

        @pl.when(tile + 1 < n_tiles)
        def _():
            x_copy(tile + 1).start()

    def matmul():
        return jnp.dot(h_ref[...], w_ref[...], preferred_element_type=F32)

    if epilogue == "plain":
        outs[0][...] = matmul().astype(BF16)
    elif epilogue == "relu2":
        outs[0][...] = jnp.square(jnp.maximum(matmul(), 0.0)).astype(BF16)
    else:
        assert epilogue == "heads_or_gelu"
        heads_ref, gated_ref = outs

        @pl.when(j < gelu_from)
        def _():
            val = (matmul() * cs_ref[...]).astype(BF16)
            for hh in range(heads_ref.shape[0]):
                heads_ref[hh] = val[:, hh * HEAD_DIM:(hh + 1) * HEAD_DIM]

        @pl.when(j >= gelu_from)
        def _():
            gated_ref[...] = _gelu_tanh(matmul()).astype(BF16)


def _norm_mod_matmul(x, norm_w, scale, shift, w, *, epilogue="plain", col_scale=None, gelu_cols=0,
                     n_lo=0, n_hi=None, tm=1024, tn=1024, vmem_mib=56):
    b, s, k = x.shape
    n_hi = w.shape[1] if n_hi is None else n_hi
    n = n_hi - n_lo
    tm = _tile(s, tm, align=8)
    tn = _tile(n, tn, divides=(n_lo, n - gelu_cols))
    j0 = n_lo // tn
    gelu_from = (n - gelu_cols) // tn
    if col_scale is None:
        col_scale = jnp.ones((n,), F32)
    per_batch = scale.shape[0] == b and b > 1
    mod_map = (lambda bb, i, j: (bb, 0, 0)) if per_batch else (lambda bb, i, j: (0, 0, 0))
    if epilogue == "heads_or_gelu":
        hpb = tn // HEAD_DIM
        out_specs = [
            pl.BlockSpec((None, hpb, tm, HEAD_DIM),
                         lambda bb, i, j: (bb, jnp.minimum(j, gelu_from - 1), i, 0)),
            pl.BlockSpec((None, tm, tn), lambda bb, i, j: (bb, i, jnp.maximum(j - gelu_from, 0))),
        ]
        out_shape = [jax.ShapeDtypeStruct((b, (n - gelu_cols) // HEAD_DIM, s, HEAD_DIM), BF16),
                     jax.ShapeDtypeStruct((b, s, gelu_cols), BF16)]
    else:
        out_specs = [pl.BlockSpec((None, tm, tn), lambda bb, i, j: (bb, i, j))]
        out_shape = [jax.ShapeDtypeStruct((b, s, n), BF16)]
    out = pl.pallas_call(
        functools.partial(_norm_mod_matmul_kernel, rows_chunk=min(NORM_ROWS, tm), epilogue=epilogue,
                          gelu_from=gelu_from),
        grid=(b, s // tm, n // tn),
        in_specs=[
            pl.BlockSpec(memory_space=pl.ANY),
            pl.BlockSpec((1, k), lambda bb, i, j: (0, 0)),
            pl.BlockSpec((None, 1, k), mod_map),
            pl.BlockSpec((None, 1, k), mod_map),
            pl.BlockSpec((k, tn), lambda bb, i, j: (0, j0 + j)),
            pl.BlockSpec((1, tn), lambda bb, i, j: (0, j)),
        ],
        out_specs=out_specs,
        out_shape=out_shape,
        scratch_shapes=[pltpu.VMEM((tm, k), F32), pltpu.VMEM((tm, k), BF16),
                        pltpu.SemaphoreType.DMA(())],
        compiler_params=_params(("arbitrary", "arbitrary", "arbitrary"), vmem_mib),
        name="norm_mod_matmul_" + epilogue,
    )(x, norm_w.reshape(1, k), scale, shift, w, col_scale.reshape(1, n))
    return out if len(out) > 1 else out[0]


def _window_structure(rows, rblk):
    kh = min(NA_KH_MAX, rows)
    kstart = min(max(rblk * ATT_QROWS - kh // 2, 0), rows - ATT_KROWS)
    out = []
    for i in range(ATT_QROWS):
        qr = rblk * ATT_QROWS + i
        r0 = min(max(qr - kh // 2, 0), rows - kh)
        row = []
        for j in range(ATT_KROWS):
            kr = kstart + j
            row.append(kr - qr + NA_KH_MAX - 1 if r0 <= kr < r0 + kh else None)
        assert sum(e is not None for e in row) == kh
        out.append(tuple(row))
    return tuple(out)


def _block_types(rows):
    nblk = rows // ATT_QROWS
    reps = [0, min(1, nblk - 1), nblk - 1]
    structs = [_window_structure(rows, r) for r in reps]
    for r in range(nblk):
        t = 0 if r == 0 else (2 if r == nblk - 1 else 1)
        assert _window_structure(rows, r) == structs[t]
    return structs


def _attn_kernel(rpb_ref, q_ref, k_ref, v_ref, kc_ref, vc_ref, *rest, rows, n_cast):
    cast_in, o_ref, cast_out = rest[:n_cast], rest[n_cast], rest[n_cast + 1:2 * n_cast + 1]
    bias_ref, s_bufs, p_bufs, l_bufs = (rest[2 * n_cast + 1], rest[2 * n_cast + 2:2 * n_cast + 4],
                                        rest[2 * n_cast + 4:2 * n_cast + 6], rest[2 * n_cast + 6:])
    for src, dst in zip(cast_in, cast_out):
        dst[...] = src[...].astype(dst.dtype)

    w = GRID_W
    n_ro = 2 * NA_KH_MAX - 1
    n_co = 2 * NA_KW - 1
    nblk = rows // ATT_QROWS
    kh = min(NA_KH_MAX, rows)
    tq = ATT_QROWS * w
    tk = ATT_KROWS * w
    h = pl.program_id(0)
    b = pl.program_id(1)

    @pl.when(b == 0)
    def _():
        qc = lax.broadcasted_iota(jnp.int32, (w, 2 * w), 0)
        lane = lax.broadcasted_iota(jnp.int32, (w, 2 * w), 1)
        kc = lane % w
        cs = jnp.clip(qc - NA_KW // 2, 0, w - NA_KW)
        col_ok = (kc >= cs) & (kc < cs + NA_KW)
        diff = kc - qc + (NA_KW - 1)
        hit = [(diff == d) & col_ok for d in range(n_co)]
        base = h * (n_ro * n_co)
        neg = jnp.full((w, 2 * w), MASK_VALUE, F32)
        tables = []
        for ro in range(n_ro):
            acc = neg
            for d in range(n_co):
                acc = jnp.where(hit[d], rpb_ref[base + ro * n_co + d] * LOG2E, acc)
            tables.append(acc)
        left = lane < w
        for t, struct in enumerate(_block_types(rows)):
            for i in range(ATT_QROWS):
                for jp in range(ATT_KROWS // 2):
                    ro_l, ro_r = struct[i][2 * jp], struct[i][2 * jp + 1]
                    blk_l = neg if ro_l is None else tables[ro_l]
                    blk_r = neg if ro_r is None else tables[ro_r]
                    blk = blk_l if ro_l == ro_r else jnp.where(left, blk_l, blk_r)
                    bias_ref[t, i * w:(i + 1) * w, jp * 2 * w:(jp + 1) * 2 * w] = blk

    nt = (((1,), (1,)), ((), ()))
    last = nblk - 1

    def key_start(r):
        kstart = jnp.clip(r * ATT_QROWS - kh // 2, 0, rows - ATT_KROWS)
        return pl.multiple_of(kstart * w, w)

    def scores(r, slot):
        r = jnp.minimum(r, last)
        q = q_ref[pl.ds(pl.multiple_of(r * tq, tq), tq), :]
        kw = k_ref[pl.ds(key_start(r), tk), :]
        t = jnp.where(r == 0, 0, jnp.where(r == last, 2, 1))
        s_bufs[slot][:, 0:tk] = lax.dot_general(q, kw, nt, preferred_element_type=F32) + bias_ref[t]
        s_bufs[slot][:, tk:] = lax.dot_general(q, kc_ref[...], nt, preferred_element_type=F32)

    def softmax(slot):
        s = s_bufs[slot][...]
        p = jnp.exp2(s - jnp.max(s, axis=-1, keepdims=True))
        l_bufs[slot][...] = 1.0 / jnp.sum(p, axis=-1, keepdims=True)
        p_bufs[slot][...] = p.astype(BF16)

    def values(r, slot):
        vw = v_ref[pl.ds(key_start(r), tk), :]
        o = (jnp.dot(p_bufs[slot][:, 0:tk], vw, preferred_element_type=F32)
             + jnp.dot(p_bufs[slot][:, tk:], vc_ref[...], preferred_element_type=F32))
        o_ref[pl.ds(pl.multiple_of(r * tq, tq), tq), :] = (o * l_bufs[slot][...]).astype(o_ref.dtype)

    scores(0, 0)
    scores(1, 1)
    softmax(0)

    def body(it, carry):
        r = 2 * it
        scores(r + 2, 0)
        softmax(1)
        values(r, 0)
        scores(r + 3, 1)
        softmax(0)
        values(r + 1, 1)
        return carry
    lax.fori_loop(0, nblk // 2, body, 0, unroll=4)


def _attention(qkv, ctx_kv, rpb, cast_weights):
    b, _, s, _ = qkv.shape
    c = ctx_kv.shape[1]
    nh, dh = NA_HEADS, HEAD_DIM
    rows = s // GRID_W
    assert s % GRID_W == 0 and rows % (2 * ATT_QROWS) == 0 and rows >= ATT_KROWS
    tq, tk = ATT_QROWS * GRID_W, ATT_KROWS * GRID_W
    cast_specs = _cast_specs(cast_weights, nh * b, lambda h, bb: (h * b + bb, 0))
    return pl.pallas_call(
        functools.partial(_attn_kernel, rows=rows, n_cast=len(cast_weights)),
        grid=(nh, b),
        in_specs=[
            pl.BlockSpec(memory_space=pltpu.SMEM),
            pl.BlockSpec((None, None, s, dh), lambda h, bb: (bb, h, 0, 0)),
            pl.BlockSpec((None, None, s, dh), lambda h, bb: (bb, nh + h, 0, 0)),
            pl.BlockSpec((None, None, s, dh), lambda h, bb: (bb, 2 * nh + h, 0, 0)),
            pl.BlockSpec((None, c, dh), lambda h, bb: (bb, 0, h)),
            pl.BlockSpec((None, c, dh), lambda h, bb: (bb, 0, nh + h)),
        ] + cast_specs,
        out_specs=[pl.BlockSpec((None, s, dh), lambda h, bb: (bb, 0, h))] + cast_specs,
        out_shape=[jax.ShapeDtypeStruct((b, s, nh * dh), BF16)]
        + [jax.ShapeDtypeStruct(wgt.shape, BF16) for wgt in cast_weights],
        scratch_shapes=[pltpu.VMEM((3, tq, tk), F32)]
        + [pltpu.VMEM((tq, tk + c), F32)] * 2 + [pltpu.VMEM((tq, tk + c), BF16)] * 2
        + [pltpu.VMEM((tq, 1), F32)] * 2,
        compiler_params=_params(("arbitrary", "arbitrary"), 56),
        name="nbr_attention",
    )(rpb.reshape(-1), qkv, qkv, qkv, ctx_kv, ctx_kv, *cast_weights)


def _mix_out_kernel(gated_hbm, ona_hbm, x_ref, gate_ref, wout_ref, ws_ref, bs_ref, snw_ref,
                    gna_ref, gsg_ref, o_ref, u_ref, g_ref, ona_ref, m_ref, sems):
    tm = u_ref.shape[0]
    d_sgu = u_ref.shape[1]
    d_na = ona_ref.shape[1]
    gd = d_sgu // SGU_GROUPS
    n_i = pl.num_programs(1)
    n_tiles = pl.num_programs(0) * n_i

    def copies(tile):
        bb, rows = tile // n_i, pl.ds((tile % n_i) * tm, tm)
        return (pltpu.make_async_copy(gated_hbm.at[bb, rows, pl.ds(0, d_sgu)], u_ref, sems.at[0]),
                pltpu.make_async_copy(gated_hbm.at[bb, rows, pl.ds(d_sgu, d_sgu)], g_ref, sems.at[1]),
                pltpu.make_async_copy(ona_hbm.at[bb, rows, :], ona_ref, sems.at[2]))

    @pl.when(pl.program_id(2) == 0)
    def _():
        tile = pl.program_id(0) * n_i + pl.program_id(1)

        @pl.when(tile == 0)
        def _():
            for cp in copies(tile):
                cp.start()

        for cp in copies(tile):
            cp.wait()

        def body(c, carry):
            r = pl.multiple_of(c * SGU_CHUNK, SGU_CHUNK)
            rs = pl.ds(r, SGU_CHUNK)
            gg = g_ref[rs, :].astype(F32)
            ms = jnp.mean(gg * gg, axis=-1, keepdims=True)
            gn = (gg * lax.rsqrt(ms + NORM_EPS) * snw_ref[...]).astype(BF16)
            parts = []
            ssq = jnp.zeros((SGU_CHUNK, 1), F32)
            for g in range(SGU_GROUPS):
                cols = slice(g * gd, (g + 1) * gd)
                mixed = jnp.dot(ws_ref[g], gn[:, cols], preferred_element_type=F32) + bs_ref[g]
                og = u_ref[rs, cols].astype(F32) * mixed
                ssq = ssq + jnp.sum(og * og, axis=-1, keepdims=True)
                parts.append(og)
            inv = lax.rsqrt(ssq / d_sgu + NORM_EPS)
            for g in range(SGU_GROUPS):
                cols = slice(g * gd, (g + 1) * gd)
                m_ref[rs, d_na + g * gd:d_na + (g + 1) * gd] = (
                    parts[g] * inv * gsg_ref[:, cols]).astype(BF16)
            on = ona_ref[rs, :].astype(F32)
            ms = jnp.mean(on * on, axis=-1, keepdims=True)
            m_ref[rs, 0:d_na] = (on * lax.rsqrt(ms + NORM_EPS) * gna_ref[...]).astype(BF16)
            return carry
        lax.fori_loop(0, tm // SGU_CHUNK, body, 0)

        @pl.when(tile + 1 < n_tiles)
        def _():
            for cp in copies(tile + 1):
                cp.start()

    acc = jnp.dot(m_ref[...], wout_ref[...], preferred_element_type=F32)
    o_ref[...] = x_ref[...] + gate_ref[...] * acc


def _mix_out(gated, o_na, x, gate, w_out, sgu_w, sgu_b, sgu_norm_w, gn_na, gn_sgu, *, tm=1024, tn=512):
    b, s, d = x.shape
    d_na = o_na.shape[2]
    d_sgu = d - d_na
    tm = _tile(s, tm)
    tn = _tile(d, tn)
    assert tm % SGU_CHUNK == 0 and gated.shape[2] == 2 * d_sgu
    const2 = lambda bb, i, j: (0, 0)
    const3 = lambda bb, i, j: (0, 0, 0)
    return pl.pallas_call(
        _mix_out_kernel,
        grid=(b, s // tm, d // tn),
        in_specs=[
            pl.BlockSpec(memory_space=pl.ANY),
            pl.BlockSpec(memory_space=pl.ANY),
            pl.BlockSpec((None, tm, tn), lambda bb, i, j: (bb, i, j)),
            pl.BlockSpec((None, 1, tn), lambda bb, i, j: (bb, 0, j)),
            pl.BlockSpec((d, tn), lambda bb, i, j: (0, j)),
            pl.BlockSpec((SGU_GROUPS, SGU_CHUNK, SGU_CHUNK), const3),
            pl.BlockSpec((SGU_GROUPS, SGU_CHUNK, 1), const3),
            pl.BlockSpec((1, d_sgu), const2),
            pl.BlockSpec((1, d_na), const2),
            pl.BlockSpec((1, d_sgu), const2),
        ],
        out_specs=pl.BlockSpec((None, tm, tn), lambda bb, i, j: (bb, i, j)),
        out_shape=jax.ShapeDtypeStruct((b, s, d), F32),
        scratch_shapes=[pltpu.VMEM((tm, d_sgu), BF16), pltpu.VMEM((tm, d_sgu), BF16),
                        pltpu.VMEM((tm, d_na), BF16), pltpu.VMEM((tm, d), BF16),
                        pltpu.SemaphoreType.DMA((3,))],
        compiler_params=_params(("arbitrary", "arbitrary", "arbitrary"), 48),
        name="mix_out",
    )(gated, o_na, x, gate, w_out, sgu_w.astype(BF16),
      sgu_b.reshape(SGU_GROUPS, SGU_CHUNK, 1), sgu_norm_w.reshape(1, d_sgu),
      gn_na.reshape(1, d_na), gn_sgu.reshape(1, d_sgu))


def _matmul_residual_kernel(a_ref, w_ref, x_ref, g_ref, o_ref, acc_ref):
    kk = pl.program_id(3)
    last = pl.num_programs(3) - 1

    def partial_product():
        return jnp.dot(a_ref[...], w_ref[...], preferred_element_type=F32)

    @pl.when(kk == 0)
    def _():
        acc_ref[...] = partial_product()

    @pl.when((kk > 0) & (kk < last))
    def _():
        acc_ref[...] += partial_product()

    @pl.when(kk == last)
    def _():
        o_ref[...] = x_ref[...] + g_ref[...] * (acc_ref[...] + partial_product())


def _matmul_residual(a, w, x, gate, *, tm=1024, tn=1024, tk=2048):
    b, s, k = a.shape
    n = w.shape[1]
    tm, tn, tk = _tile(s, tm), _tile(n, tn), _tile(k, tk)
    assert k // tk >= 2
    return pl.pallas_call(
        _matmul_residual_kernel,
        grid=(b, s // tm, n // tn, k // tk),
        in_specs=[
            pl.BlockSpec((None, tm, tk), lambda bb, i, j, kk: (bb, i, kk)),
            pl.BlockSpec((tk, tn), lambda bb, i, j, kk: (kk, j)),
            pl.BlockSpec((None, tm, tn), lambda bb, i, j, kk: (bb, i, j)),
            pl.BlockSpec((None, 1, tn), lambda bb, i, j, kk: (bb, 0, j)),
        ],
        out_specs=pl.BlockSpec((None, tm, tn), lambda bb, i, j, kk: (bb, i, j)),
        out_shape=jax.ShapeDtypeStruct((b, s, n), F32),
        scratch_shapes=[pltpu.VMEM((tm, tn), F32)],
        compiler_params=_params(("parallel", "parallel", "parallel", "arbitrary"), 48),
        name="ff2_residual",
    )(a, w, x, gate)


def _rmsnorm_kernel(x_ref, w_ref, o_ref):
    x = x_ref[...]
    ms = jnp.mean(x * x, axis=-1, keepdims=True)
    o_ref[...] = x * lax.rsqrt(ms + NORM_EPS) * w_ref[...]


def _rmsnorm(x, w, *, tm=256):
    b, s, d = x.shape
    tm = _tile(s, tm)
    return pl.pallas_call(
        _rmsnorm_kernel,
        grid=(b, s // tm),
        in_specs=[pl.BlockSpec((None, tm, d), lambda bb, i: (bb, i, 0)),
                  pl.BlockSpec((1, d), lambda bb, i: (0, 0))],
        out_specs=pl.BlockSpec((None, tm, d), lambda bb, i: (bb, i, 0)),
        out_shape=jax.ShapeDtypeStruct((b, s, d), F32),
        compiler_params=_params(("parallel", "parallel"), 32),
        name="final_rmsnorm",
    )(x, w.reshape(1, d))


def kernel(x, c, ctx, c_ctx, w_ada, b_ada, norm1_w, w_in, rpb, sgu_norm_w, sgu_w, sgu_b,
           grp_norm_na, grp_norm_sgu, w_out, norm2_w, w_ff1, w_ff2, final_norm_w):
    b, s, d = x.shape
    n_ctx = ctx.shape[1]
    d_na = NA_HEADS * HEAD_DIM
    d_sgu = d - d_na
    depth = w_ada.shape[0]
    assert depth == 1

    n_rows = -(-(b + 1) // 8) * 8
    cvecs = jnp.zeros((n_rows, d), F32).at[:b].set(c).at[b].set(c_ctx)
    mod = _adaln(cvecs, w_ada[0], b_ada[0]).reshape(n_rows, N_MOD, d)
    lat = lambda i: mod[:b, i].reshape(b, 1, d)
    cx = lambda i: mod[b, i].reshape(1, 1, d)

    w_in_b = w_in[0].astype(BF16)
    qkv_scale = jnp.concatenate([jnp.full((d_na,), HEAD_DIM ** -0.5 * LOG2E, F32),
                                 jnp.ones((2 * d_na + 2 * d_sgu,), F32)])
    qkv, gated = _norm_mod_matmul(x, norm1_w[0], lat(1), lat(0), w_in_b, epilogue="heads_or_gelu",
                                  col_scale=qkv_scale, gelu_cols=2 * d_sgu, vmem_mib=60)
    ctx_kv = _norm_mod_matmul(ctx.reshape(1, b * n_ctx, d), norm1_w[0], cx(1), cx(0), w_in_b,
                              n_lo=d_na, n_hi=3 * d_na).reshape(b, n_ctx, 2 * d_na)
    o_na, w_out_b, w_ff1_b, w_ff2_b = _attention(qkv, ctx_kv, rpb[0], [w_out[0], w_ff1[0], w_ff2[0]])
    x1 = _mix_out(gated, o_na, x, lat(2), w_out_b, sgu_w[0], sgu_b[0], sgu_norm_w[0],
                  grp_norm_na[0], grp_norm_sgu[0])
    hidden = _norm_mod_matmul(x1, norm2_w[0], lat(4), lat(3), w_ff1_b, epilogue="relu2")
    x2 = _matmul_residual(hidden, w_ff2_b, x1, lat(5))
    return _rmsnorm(x2, final_norm_w)
```

```python
import functools
import math

import jax
import jax.numpy as jnp
from jax import lax
from jax.experimental import pallas as pl
from jax.experimental.pallas import tpu as pltpu

F32 = jnp.float32
BF16 = jnp.bfloat16

GRID_W = 64
NA_HEADS = 16
HEAD_DIM = 128
NA_KH_MAX = 8
NA_KW = 16
SGU_GROUPS = 4
SGU_CHUNK = 128
N_MOD = 6
NORM_EPS = 1e-6
MASK_VALUE = -1e30
LOG2E = math.log2(math.e)

ATT_QROWS = 4
ATT_KROWS = 12
NORM_ROWS = 16
NORM_UNROLL = 4
FF1_RING = 4
MIX_RING = 4

V7X_VMEM_BYTES = 64 << 20


def _params(semantics, vmem_mib):
    assert (vmem_mib << 20) < V7X_VMEM_BYTES
    return pltpu.CompilerParams(dimension_semantics=semantics, vmem_limit_bytes=vmem_mib << 20)


def _tile(n, pref, align=128, divides=()):
    for t in range(min(pref, n) // align * align, 0, -align):
        if n % t == 0 and all(v % t == 0 for v in divides):
            return t
    assert not any(divides)
    return n


def _cast_specs(weights, steps, step_index_map):
    specs = []
    for wgt in weights:
        slab = wgt.shape[0] // steps
        assert wgt.shape[0] % steps == 0 and slab % 16 == 0
        specs.append(pl.BlockSpec((slab, wgt.shape[1]), step_index_map))
    return specs


def _adaln_kernel(c_ref, w_ref, b_ref, o_ref):
    c = c_ref[...]
    a = (c * jax.nn.sigmoid(c)).astype(BF16)
    o_ref[...] = jnp.dot(a, w_ref[...].astype(BF16), preferred_element_type=F32) + b_ref[...]


def _adaln(cvecs, w_ada, b_ada):
    r, d = cvecs.shape
    n = w_ada.shape[1]
    tn = _tile(n, 512)
    return pl.pallas_call(
        _adaln_kernel,
        grid=(n // tn,),
        in_specs=[
            pl.BlockSpec((r, d), lambda j: (0, 0)),
            pl.BlockSpec((d, tn), lambda j: (0, j)),
            pl.BlockSpec((1, tn), lambda j: (0, j)),
        ],
        out_specs=pl.BlockSpec((r, tn), lambda j: (0, j)),
        out_shape=jax.ShapeDtypeStruct((r, n), F32),
        compiler_params=_params(("arbitrary",), 40),
        name="adaln",
    )(cvecs, w_ada, b_ada.reshape(1, n))


def _gelu_tanh(x):
    a = -2.0 * math.sqrt(2.0 / math.pi) * LOG2E
    return x / (1.0 + jnp.exp2(x * (a + (a * 0.044715) * (x * x))))


def _norm_mod_matmul_kernel(x_hbm, nw_ref, sc_ref, sh_ref, w_ref, cs_ref, *rest,
                            rows_chunk, epilogue, gelu_from):
    outs, (x_buf, h_ref, x_sem) = rest[:-3], rest[-3:]
    tm = x_buf.shape[0]
    bb = pl.program_id(0)
    i = pl.program_id(1)
    j = pl.program_id(2)
    n_i = pl.num_programs(1)
    n_tiles = pl.num_programs(0) * n_i

    def x_copy(tile):
        return pltpu.make_async_copy(
            x_hbm.at[tile // n_i, pl.ds((tile % n_i) * tm, tm), :], x_buf, x_sem)

    @pl.when(j == 0)
    def _():
        tile = bb * n_i + i

        @pl.when(tile == 0)
        def _():
            x_copy(tile).start()

        x_copy(tile).wait()

        def body(c, carry):
            r = pl.multiple_of(c * rows_chunk, rows_chunk)
            x = x_buf[pl.ds(r, rows_chunk), :]
            ms = jnp.mean(x * x, axis=-1, keepdims=True)
            gain = nw_ref[...] * (1.0 + sc_ref[...])
            h = x * lax.rsqrt(ms + NORM_EPS) * gain + sh_ref[...]
            h_ref[pl.ds(r, rows_chunk), :] = h.astype(BF16)
            return carry
        lax.fori_loop(0, tm // rows_chunk, body, 0, unroll=NORM_UNROLL)

        @pl.when(tile + 1 < n_tiles)
        def _():
            x_copy(tile + 1).start()

    def matmul():
        return jnp.dot(h_ref[...], w_ref[...], preferred_element_type=F32)

    if epilogue == "plain":
        outs[0][...] = matmul().astype(BF16)
    else:
        assert epilogue == "heads_or_gelu"
        heads_ref, gated_ref = outs

        @pl.when(j < gelu_from)
        def _():
            val = (matmul() * cs_ref[...]).astype(BF16)
            for hh in range(heads_ref.shape[0]):
                heads_ref[hh] = val[:, hh * HEAD_DIM:(hh + 1) * HEAD_DIM]

        @pl.when(j >= gelu_from)
        def _():
            gated_ref[...] = _gelu_tanh(matmul()).astype(BF16)


def _norm_mod_matmul(x, norm_w, scale, shift, w, *, epilogue="plain", col_scale=None, gelu_cols=0,
                     n_lo=0, n_hi=None, tm=1024, tn=1024, vmem_mib=56):
    b, s, k = x.shape
    n_hi = w.shape[1] if n_hi is None else n_hi
    n = n_hi - n_lo
    tm = _tile(s, tm, align=8)
    tn = _tile(n, tn, divides=(n_lo, n - gelu_cols))
    j0 = n_lo // tn
    gelu_from = (n - gelu_cols) // tn
    if col_scale is None:
        col_scale = jnp.ones((n,), F32)
    per_batch = scale.shape[0] == b and b > 1
    mod_map = (lambda bb, i, j: (bb, 0, 0)) if per_batch else (lambda bb, i, j: (0, 0, 0))
    if epilogue == "heads_or_gelu":
        hpb = tn // HEAD_DIM
        out_specs = [
            pl.BlockSpec((None, hpb, tm, HEAD_DIM),
                         lambda bb, i, j: (bb, jnp.minimum(j, gelu_from - 1), i, 0)),
            pl.BlockSpec((None, tm, tn), lambda bb, i, j: (bb, i, jnp.maximum(j - gelu_from, 0))),
        ]
        out_shape = [jax.ShapeDtypeStruct((b, (n - gelu_cols) // HEAD_DIM, s, HEAD_DIM), BF16),
                     jax.ShapeDtypeStruct((b, s, gelu_cols), BF16)]
    else:
        out_specs = [pl.BlockSpec((None, tm, tn), lambda bb, i, j: (bb, i, j))]
        out_shape = [jax.ShapeDtypeStruct((b, s, n), BF16)]
    out = pl.pallas_call(
        functools.partial(_norm_mod_matmul_kernel, rows_chunk=min(NORM_ROWS, tm), epilogue=epilogue,
                          gelu_from=gelu_from),
        grid=(b, s // tm, n // tn),
        in_specs=[
            pl.BlockSpec(memory_space=pl.ANY),
            pl.BlockSpec((1, k), lambda bb, i, j: (0, 0)),
            pl.BlockSpec((None, 1, k), mod_map),
            pl.BlockSpec((None, 1, k), mod_map),
            pl.BlockSpec((k, tn), lambda bb, i, j: (0, j0 + j)),
            pl.BlockSpec((1, tn), lambda bb, i, j: (0, j)),
        ],
        out_specs=out_specs,
        out_shape=out_shape,
        scratch_shapes=[pltpu.VMEM((tm, k), F32), pltpu.VMEM((tm, k), BF16),
                        pltpu.SemaphoreType.DMA(())],
        compiler_params=_params(("arbitrary", "arbitrary", "arbitrary"), vmem_mib),
        name="norm_mod_matmul_" + epilogue,
    )(x, norm_w.reshape(1, k), scale, shift, w, col_scale.reshape(1, n))
    return out if len(out) > 1 else out[0]


def _ff1_kernel(x_hbm, nw_ref, sc_ref, sh_ref, w_ref, o_ref, x_ring, h_even, h_odd, sems, *,
                n_i, n_j, n_tiles, rows_chunk, per_batch):
    tm = h_even.shape[0]
    rs = tm // n_j
    total = n_tiles * n_j
    tile = pl.program_id(0) * n_i + pl.program_id(1)
    j = pl.program_id(2)

    def chunk_copy(g):
        t, c = g // n_j, g % n_j
        return pltpu.make_async_copy(
            x_hbm.at[t // n_i, pl.ds((t % n_i) * tm + c * rs, rs), :],
            x_ring.at[g % FF1_RING], sems.at[g % FF1_RING])

    def norm_chunk(g, dst):
        t, c = g // n_j, g % n_j
        bm = jnp.minimum(t, n_tiles - 1) // n_i if per_batch else 0
        gain = nw_ref[...] * (1.0 + sc_ref[bm])
        shift = sh_ref[bm]
        for sub in range(rs // rows_chunk):
            x = x_ring[g % FF1_RING, sub * rows_chunk:(sub + 1) * rows_chunk, :]
            ms = jnp.mean(x * x, axis=-1, keepdims=True)
            h = x * lax.rsqrt(ms + NORM_EPS) * gain + shift
            r = pl.multiple_of(c * rs + sub * rows_chunk, rows_chunk)
            dst[pl.ds(r, rows_chunk), :] = h.astype(BF16)

    @pl.when((tile == 0) & (j == 0))
    def _():
        for g in range(min(FF1_RING, total)):
            chunk_copy(g).start()
        for g in range(n_j):
            chunk_copy(g).wait()
            norm_chunk(g, h_even)
            if g + FF1_RING < total:
                chunk_copy(g + FF1_RING).start()

    g = (tile + 1) * n_j + j

    @pl.when(g < total)
    def _():
        chunk_copy(g).wait()

    def step(h_cur, h_next):
        norm_chunk(g, h_next)
        acc = jnp.dot(h_cur[...], w_ref[...], preferred_element_type=F32)
        o_ref[...] = jnp.square(jnp.maximum(acc, 0.0)).astype(o_ref.dtype)

    @pl.when(tile % 2 == 0)
    def _():
        step(h_even, h_odd)

    @pl.when(tile % 2 == 1)
    def _():
        step(h_odd, h_even)

    @pl.when(g + FF1_RING < total)
    def _():
        chunk_copy(g + FF1_RING).start()


def _ff1(x, norm_w, scale, shift, w, *, tm=1024, tn=1024):
    b, s, k = x.shape
    n = w.shape[1]
    tm = _tile(s, tm, align=8)
    tn = _tile(n, tn)
    n_i, n_j = s // tm, n // tn
    rs = tm // n_j
    rows_chunk = min(NORM_ROWS, rs)
    assert tm % n_j == 0 and rs % rows_chunk == 0 and n_j % FF1_RING == 0
    nb = scale.shape[0]
    whole = lambda bb, i, j: (0, 0, 0)
    return pl.pallas_call(
        functools.partial(_ff1_kernel, n_i=n_i, n_j=n_j, n_tiles=b * n_i, rows_chunk=rows_chunk,
                          per_batch=nb == b and b > 1),
        grid=(b, n_i, n_j),
        in_specs=[
            pl.BlockSpec(memory_space=pl.ANY),
            pl.BlockSpec((1, k), lambda bb, i, j: (0, 0)),
            pl.BlockSpec((nb, 1, k), whole),
            pl.BlockSpec((nb, 1, k), whole),
            pl.BlockSpec((k, tn), lambda bb, i, j: (0, j)),
        ],
        out_specs=pl.BlockSpec((None, tm, tn), lambda bb, i, j: (bb, i, j)),
        out_shape=jax.ShapeDtypeStruct((b, s, n), BF16),
        scratch_shapes=[pltpu.VMEM((FF1_RING, rs, k), F32), pltpu.VMEM((tm, k), BF16),
                        pltpu.VMEM((tm, k), BF16), pltpu.SemaphoreType.DMA((FF1_RING,))],
        compiler_params=_params(("arbitrary", "arbitrary", "arbitrary"), 56),
        name="ff1_norm_matmul_relu2",
    )(x, norm_w.reshape(1, k), scale, shift, w)


def _window_structure(rows, rblk):
    kh = min(NA_KH_MAX, rows)
    kstart = min(max(rblk * ATT_QROWS - kh // 2, 0), rows - ATT_KROWS)
    out = []
    for i in range(ATT_QROWS):
        qr = rblk * ATT_QROWS + i
        r0 = min(max(qr - kh // 2, 0), rows - kh)
        row = []
        for j in range(ATT_KROWS):
            kr = kstart + j
            row.append(kr - qr + NA_KH_MAX - 1 if r0 <= kr < r0 + kh else None)
        assert sum(e is not None for e in row) == kh
        out.append(tuple(row))
    return tuple(out)


def _block_types(rows):
    nblk = rows // ATT_QROWS
    reps = [0, min(1, nblk - 1), nblk - 1]
    structs = [_window_structure(rows, r) for r in reps]
    for r in range(nblk):
        t = 0 if r == 0 else (2 if r == nblk - 1 else 1)
        assert _window_structure(rows, r) == structs[t]
    return structs


def _attn_kernel(rpb_ref, q_ref, k_ref, v_ref, kc_ref, vc_ref, *rest, rows, n_cast):
    cast_in, o_ref, cast_out = rest[:n_cast], rest[n_cast], rest[n_cast + 1:2 * n_cast + 1]
    bias_ref, s_bufs, p_bufs, l_bufs = (rest[2 * n_cast + 1], rest[2 * n_cast + 2:2 * n_cast + 4],
                                        rest[2 * n_cast + 4:2 * n_cast + 6], rest[2 * n_cast + 6:])
    for src, dst in zip(cast_in, cast_out):
        dst[...] = src[...].astype(dst.dtype)

    w = GRID_W
    n_ro = 2 * NA_KH_MAX - 1
    n_co = 2 * NA_KW - 1
    nblk = rows // ATT_QROWS
    kh = min(NA_KH_MAX, rows)
    tq = ATT_QROWS * w
    tk = ATT_KROWS * w
    h = pl.program_id(0)
    b = pl.program_id(1)

    @pl.when(b == 0)
    def _():
        qc = lax.broadcasted_iota(jnp.int32, (w, 2 * w), 0)
        lane = lax.broadcasted_iota(jnp.int32, (w, 2 * w), 1)
        kc = lane % w
        cs = jnp.clip(qc - NA_KW // 2, 0, w - NA_KW)
        col_ok = (kc >= cs) & (kc < cs + NA_KW)
        diff = kc - qc + (NA_KW - 1)
        hit = [(diff == d) & col_ok for d in range(n_co)]
        base = h * (n_ro * n_co)
        neg = jnp.full((w, 2 * w), MASK_VALUE, F32)
        tables = []
        for ro in range(n_ro):
            acc = neg
            for d in range(n_co):
                acc = jnp.where(hit[d], rpb_ref[base + ro * n_co + d] * LOG2E, acc)
            tables.append(acc)
        left = lane < w
        for t, struct in enumerate(_block_types(rows)):
            for i in range(ATT_QROWS):
                for jp in range(ATT_KROWS // 2):
                    ro_l, ro_r = struct[i][2 * jp], struct[i][2 * jp + 1]
                    blk_l = neg if ro_l is None else tables[ro_l]
                    blk_r = neg if ro_r is None else tables[ro_r]
                    blk = blk_l if ro_l == ro_r else jnp.where(left, blk_l, blk_r)
                    bias_ref[t, i * w:(i + 1) * w, jp * 2 * w:(jp + 1) * 2 * w] = blk

    nt = (((1,), (1,)), ((), ()))
    last = nblk - 1

    def key_start(r):
        kstart = jnp.clip(r * ATT_QROWS - kh // 2, 0, rows - ATT_KROWS)
        return pl.multiple_of(kstart * w, w)

    def scores(r, slot):
        r = jnp.minimum(r, last)
        q = q_ref[pl.ds(pl.multiple_of(r * tq, tq), tq), :]
        kw = k_ref[pl.ds(key_start(r), tk), :]
        t = jnp.where(r == 0, 0, jnp.where(r == last, 2, 1))
        s_bufs[slot][:, 0:tk] = lax.dot_general(q, kw, nt, preferred_element_type=F32) + bias_ref[t]
        s_bufs[slot][:, tk:] = lax.dot_general(q, kc_ref[...], nt, preferred_element_type=F32)

    def softmax(slot):
        s = s_bufs[slot][...]
        p = jnp.exp2(s - jnp.max(s, axis=-1, keepdims=True))
        l_bufs[slot][...] = 1.0 / jnp.sum(p, axis=-1, keepdims=True)
        p_bufs[slot][...] = p.astype(BF16)

    def values(r, slot):
        vw = v_ref[pl.ds(key_start(r), tk), :]
        o = (jnp.dot(p_bufs[slot][:, 0:tk], vw, preferred_element_type=F32)
             + jnp.dot(p_bufs[slot][:, tk:], vc_ref[...], preferred_element_type=F32))
        o_ref[pl.ds(pl.multiple_of(r * tq, tq), tq), :] = (o * l_bufs[slot][...]).astype(o_ref.dtype)

    scores(0, 0)
    scores(1, 1)
    softmax(0)

    def body(it, carry):
        r = 2 * it
        scores(r + 2, 0)
        softmax(1)
        values(r, 0)
        scores(r + 3, 1)
        softmax(0)
        values(r + 1, 1)
        return carry
    lax.fori_loop(0, nblk // 2, body, 0, unroll=4)


def _attention(qkv, ctx_kv, rpb, cast_weights):
    b, _, s, _ = qkv.shape
    c = ctx_kv.shape[1]
    nh, dh = NA_HEADS, HEAD_DIM
    rows = s // GRID_W
    assert s % GRID_W == 0 and rows % (2 * ATT_QROWS) == 0 and rows >= ATT_KROWS
    tq, tk = ATT_QROWS * GRID_W, ATT_KROWS * GRID_W
    cast_specs = _cast_specs(cast_weights, nh * b, lambda h, bb: (h * b + bb, 0))
    return pl.pallas_call(
        functools.partial(_attn_kernel, rows=rows, n_cast=len(cast_weights)),
        grid=(nh, b),
        in_specs=[
            pl.BlockSpec(memory_space=pltpu.SMEM),
            pl.BlockSpec((None, None, s, dh), lambda h, bb: (bb, h, 0, 0)),
            pl.BlockSpec((None, None, s, dh), lambda h, bb: (bb, nh + h, 0, 0)),
            pl.BlockSpec((None, None, s, dh), lambda h, bb: (bb, 2 * nh + h, 0, 0)),
            pl.BlockSpec((None, c, dh), lambda h, bb: (bb, 0, h)),
            pl.BlockSpec((None, c, dh), lambda h, bb: (bb, 0, nh + h)),
        ] + cast_specs,
        out_specs=[pl.BlockSpec((None, s, dh), lambda h, bb: (bb, 0, h))] + cast_specs,
        out_shape=[jax.ShapeDtypeStruct((b, s, nh * dh), BF16)]
        + [jax.ShapeDtypeStruct(wgt.shape, BF16) for wgt in cast_weights],
        scratch_shapes=[pltpu.VMEM((3, tq, tk), F32)]
        + [pltpu.VMEM((tq, tk + c), F32)] * 2 + [pltpu.VMEM((tq, tk + c), BF16)] * 2
        + [pltpu.VMEM((tq, 1), F32)] * 2,
        compiler_params=_params(("arbitrary", "arbitrary"), 56),
        name="nbr_attention",
    )(rpb.reshape(-1), qkv, qkv, qkv, ctx_kv, ctx_kv, *cast_weights)


def _mix_out_kernel(gated_hbm, ona_hbm, x_ref, gate_ref, wout_ref, ws_ref, bs_ref, snw_ref,
                    gna_ref, gsg_ref, o_ref, u_ring, g_ring, ona_ring, m_even, m_odd, sems, *,
                    n_i, n_j, n_tiles):
    tm = m_even.shape[0]
    d_sgu = u_ring.shape[2]
    d_na = ona_ring.shape[2]
    gd = d_sgu // SGU_GROUPS
    total = n_tiles * n_j
    tile = pl.program_id(0) * n_i + pl.program_id(1)
    j = pl.program_id(2)

    def copies(g):
        t, c, slot = g // n_j, g % n_j, g % MIX_RING
        bb, rows = t // n_i, pl.ds((t % n_i) * tm + c * SGU_CHUNK, SGU_CHUNK)
        return (pltpu.make_async_copy(gated_hbm.at[bb, rows, pl.ds(0, d_sgu)], u_ring.at[slot],
                                      sems.at[0, slot]),
                pltpu.make_async_copy(gated_hbm.at[bb, rows, pl.ds(d_sgu, d_sgu)], g_ring.at[slot],
                                      sems.at[1, slot]),
                pltpu.make_async_copy(ona_hbm.at[bb, rows, :], ona_ring.at[slot], sems.at[2, slot]))

    def mix_chunk(g, dst):
        slot = g % MIX_RING
        rs = pl.ds(pl.multiple_of((g % n_j) * SGU_CHUNK, SGU_CHUNK), SGU_CHUNK)
        gg = g_ring[slot].astype(F32)
        ms = jnp.mean(gg * gg, axis=-1, keepdims=True)
        gn = (gg * lax.rsqrt(ms + NORM_EPS) * snw_ref[...]).astype(BF16)
        parts = []
        ssq = jnp.zeros((SGU_CHUNK, 1), F32)
        for grp in range(SGU_GROUPS):
            cols = slice(grp * gd, (grp + 1) * gd)
            mixed = jnp.dot(ws_ref[grp], gn[:, cols], preferred_element_type=F32) + bs_ref[grp]
            og = u_ring[slot, :, cols].astype(F32) * mixed
            ssq = ssq + jnp.sum(og * og, axis=-1, keepdims=True)
            parts.append(og)
        inv = lax.rsqrt(ssq / d_sgu + NORM_EPS)
        for grp in range(SGU_GROUPS):
            cols = slice(grp * gd, (grp + 1) * gd)
            dst[rs, d_na + grp * gd:d_na + (grp + 1) * gd] = (
                parts[grp] * inv * gsg_ref[:, cols]).astype(BF16)
        on = ona_ring[slot].astype(F32)
        ms = jnp.mean(on * on, axis=-1, keepdims=True)
        dst[rs, 0:d_na] = (on * lax.rsqrt(ms + NORM_EPS) * gna_ref[...]).astype(BF16)

    @pl.when((tile == 0) & (j == 0))
    def _():
        for g in range(min(MIX_RING, total)):
            for cp in copies(g):
                cp.start()
        for g in range(n_j):
            for cp in copies(g):
                cp.wait()
            mix_chunk(g, m_even)
            if g + MIX_RING < total:
                for cp in copies(g + MIX_RING):
                    cp.start()

    g = (tile + 1) * n_j + j

    @pl.when(g < total)
    def _():
        for cp in copies(g):
            cp.wait()

    def step(m_cur, m_next):
        mix_chunk(g, m_next)
        acc = jnp.dot(m_cur[...], wout_ref[...], preferred_element_type=F32)
        o_ref[...] = x_ref[...] + gate_ref[...] * acc

    @pl.when(tile % 2 == 0)
    def _():
        step(m_even, m_odd)

    @pl.when(tile % 2 == 1)
    def _():
        step(m_odd, m_even)

    @pl.when(g + MIX_RING < total)
    def _():
        for cp in copies(g + MIX_RING):
            cp.start()


def _mix_out(gated, o_na, x, gate, w_out, sgu_w, sgu_b, sgu_norm_w, gn_na, gn_sgu, *, tm=1024):
    b, s, d = x.shape
    d_na = o_na.shape[2]
    d_sgu = d - d_na
    tm = _tile(s, tm)
    n_i, n_j = s // tm, tm // SGU_CHUNK
    tn = d // n_j
    assert (tm % SGU_CHUNK == 0 and d % n_j == 0 and tn % 128 == 0 and n_j % MIX_RING == 0
            and gated.shape[2] == 2 * d_sgu)
    const2 = lambda bb, i, j: (0, 0)
    const3 = lambda bb, i, j: (0, 0, 0)
    return pl.pallas_call(
        functools.partial(_mix_out_kernel, n_i=n_i, n_j=n_j, n_tiles=b * n_i),
        grid=(b, n_i, n_j),
        in_specs=[
            pl.BlockSpec(memory_space=pl.ANY),
            pl.BlockSpec(memory_space=pl.ANY),
            pl.BlockSpec((None, tm, tn), lambda bb, i, j: (bb, i, j)),
            pl.BlockSpec((None, 1, tn), lambda bb, i, j: (bb, 0, j)),
            pl.BlockSpec((d, tn), lambda bb, i, j: (0, j)),
            pl.BlockSpec((SGU_GROUPS, SGU_CHUNK, SGU_CHUNK), const3),
            pl.BlockSpec((SGU_GROUPS, SGU_CHUNK, 1), const3),
            pl.BlockSpec((1, d_sgu), const2),
            pl.BlockSpec((1, d_na), const2),
            pl.BlockSpec((1, d_sgu), const2),
        ],
        out_specs=pl.BlockSpec((None, tm, tn), lambda bb, i, j: (bb, i, j)),
        out_shape=jax.ShapeDtypeStruct((b, s, d), F32),
        scratch_shapes=[pltpu.VMEM((MIX_RING, SGU_CHUNK, d_sgu), BF16),
                        pltpu.VMEM((MIX_RING, SGU_CHUNK, d_sgu), BF16),
                        pltpu.VMEM((MIX_RING, SGU_CHUNK, d_na), BF16),
                        pltpu.VMEM((tm, d), BF16), pltpu.VMEM((tm, d), BF16),
                        pltpu.SemaphoreType.DMA((3, MIX_RING))],
        compiler_params=_params(("arbitrary", "arbitrary", "arbitrary"), 48),
        name="mix_out",
    )(gated, o_na, x, gate, w_out, sgu_w.astype(BF16),
      sgu_b.reshape(SGU_GROUPS, SGU_CHUNK, 1), sgu_norm_w.reshape(1, d_sgu),
      gn_na.reshape(1, d_na), gn_sgu.reshape(1, d_sgu))


def _matmul_residual_kernel(a_ref, w_ref, x_ref, g_ref, o_ref, acc_ref):
    kk = pl.program_id(3)
    last = pl.num_programs(3) - 1

    def partial_product():
        return jnp.dot(a_ref[...], w_ref[...], preferred_element_type=F32)

    @pl.when(kk == 0)
    def _():
        acc_ref[...] = partial_product()

    @pl.when((kk > 0) & (kk < last))
    def _():
        acc_ref[...] += partial_product()

    @pl.when(kk == last)
    def _():
        o_ref[...] = x_ref[...] + g_ref[...] * (acc_ref[...] + partial_product())


def _matmul_residual(a, w, x, gate, *, tm=1024, tn=1024, tk=2048):
    b, s, k = a.shape
    n = w.shape[1]
    tm, tn, tk = _tile(s, tm), _tile(n, tn), _tile(k, tk)
    assert k // tk >= 2
    return pl.pallas_call(
        _matmul_residual_kernel,
        grid=(b, s // tm, n // tn, k // tk),
        in_specs=[
            pl.BlockSpec((None, tm, tk), lambda bb, i, j, kk: (bb, i, kk)),
            pl.BlockSpec((tk, tn), lambda bb, i, j, kk: (kk, j)),
            pl.BlockSpec((None, tm, tn), lambda bb, i, j, kk: (bb, i, j)),
            pl.BlockSpec((None, 1, tn), lambda bb, i, j, kk: (bb, 0, j)),
        ],
        out_specs=pl.BlockSpec((None, tm, tn), lambda bb, i, j, kk: (bb, i, j)),
        out_shape=jax.ShapeDtypeStruct((b, s, n), F32),
        scratch_shapes=[pltpu.VMEM((tm, tn), F32)],
        compiler_params=_params(("parallel", "parallel", "parallel", "arbitrary"), 48),
        name="ff2_residual",
    )(a, w, x, gate)


def _rmsnorm_kernel(x_ref, w_ref, o_ref):
    x = x_ref[...]
    ms = jnp.mean(x * x, axis=-1, keepdims=True)
    o_ref[...] = x * lax.rsqrt(ms + NORM_EPS) * w_ref[...]


def _rmsnorm(x, w, *, tm=256):
    b, s, d = x.shape
    tm = _tile(s, tm)
    return pl.pallas_call(
        _rmsnorm_kernel,
        grid=(b, s // tm),
        in_specs=[pl.BlockSpec((None, tm, d), lambda bb, i: (bb, i, 0)),
                  pl.BlockSpec((1, d), lambda bb, i: (0, 0))],
        out_specs=pl.BlockSpec((None, tm, d), lambda bb, i: (bb, i, 0)),
        out_shape=jax.ShapeDtypeStruct((b, s, d), F32),
        compiler_params=_params(("parallel", "parallel"), 32),
        name="final_rmsnorm",
    )(x, w.reshape(1, d))


def kernel(x, c, ctx, c_ctx, w_ada, b_ada, norm1_w, w_in, rpb, sgu_norm_w, sgu_w, sgu_b,
           grp_norm_na, grp_norm_sgu, w_out, norm2_w, w_ff1, w_ff2, final_norm_w):
    b, s, d = x.shape
    n_ctx = ctx.shape[1]
    d_na = NA_HEADS * HEAD_DIM
    d_sgu = d - d_na
    depth = w_ada.shape[0]
    assert depth == 1

    n_rows = -(-(b + 1) // 8) * 8
    cvecs = jnp.zeros((n_rows, d), F32).at[:b].set(c).at[b].set(c_ctx)
    mod = _adaln(cvecs, w_ada[0], b_ada[0]).reshape(n_rows, N_MOD, d)
    lat = lambda i: mod[:b, i].reshape(b, 1, d)
    cx = lambda i: mod[b, i].reshape(1, 1, d)

    w_in_b = w_in[0].astype(BF16)
    qkv_scale = jnp.concatenate([jnp.full((d_na,), HEAD_DIM ** -0.5 * LOG2E, F32),
                                 jnp.ones((2 * d_na + 2 * d_sgu,), F32)])
    qkv, gated = _norm_mod_matmul(x, norm1_w[0], lat(1), lat(0), w_in_b, epilogue="heads_or_gelu",
                                  col_scale=qkv_scale, gelu_cols=2 * d_sgu, vmem_mib=60)
    ctx_kv = _norm_mod_matmul(ctx.reshape(1, b * n_ctx, d), norm1_w[0], cx(1), cx(0), w_in_b,
                              n_lo=d_na, n_hi=3 * d_na).reshape(b, n_ctx, 2 * d_na)
    o_na, w_out_b, w_ff1_b, w_ff2_b = _attention(qkv, ctx_kv, rpb[0], [w_out[0], w_ff1[0], w_ff2[0]])
    x1 = _mix_out(gated, o_na, x, lat(2), w_out_b, sgu_w[0], sgu_b[0], sgu_norm_w[0],
                  grp_norm_na[0], grp_norm_sgu[0])
    hidden = _ff1(x1, norm2_w[0], lat(4), lat(3), w_ff1_b)
    x2 = _matmul_residual(hidden, w_ff2_b, x1, lat(5))
    return _rmsnorm(x2, final_norm_w)
```

```python
import functools
import math

import jax
import jax.numpy as jnp
from jax import lax
from jax.experimental import pallas as pl
from jax.experimental.pallas import tpu as pltpu

F32 = jnp.float32
BF16 = jnp.bfloat16

GRID_W = 64
NA_HEADS = 16
HEAD_DIM = 128
NA_KH_MAX = 8
NA_KW = 16
SGU_GROUPS = 4
SGU_CHUNK = 128
N_MOD = 6
NORM_EPS = 1e-6
MASK_VALUE = -1e30
LOG2E = math.log2(math.e)

ATT_QROWS = 4
ATT_KROWS = 12
NORM_ROWS = 16
NORM_UNROLL = 4
FF1_RING = 4
MIX_RING = 4

V7X_VMEM_BYTES = 64 << 20


def _params(semantics, vmem_mib):
    assert (vmem_mib << 20) < V7X_VMEM_BYTES
    return pltpu.CompilerParams(dimension_semantics=semantics, vmem_limit_bytes=vmem_mib << 20)


def _tile(n, pref, align=128, divides=()):
    for t in range(min(pref, n) // align * align, 0, -align):
        if n % t == 0 and all(v % t == 0 for v in divides):
            return t
    assert not any(divides)
    return n


def _cast_specs(weights, steps, step_index_map):
    specs = []
    for wgt in weights:
        slab = wgt.shape[0] // steps
        assert wgt.shape[0] % steps == 0 and slab % 16 == 0
        specs.append(pl.BlockSpec((slab, wgt.shape[1]), step_index_map))
    return specs


def _adaln_kernel(c_ref, w_ref, b_ref, o_ref):
    c = c_ref[...]
    a = (c * jax.nn.sigmoid(c)).astype(BF16)
    o_ref[...] = jnp.dot(a, w_ref[...].astype(BF16), preferred_element_type=F32) + b_ref[...]


def _adaln(cvecs, w_ada, b_ada):
    r, d = cvecs.shape
    n = w_ada.shape[1]
    tn = _tile(n, 512)
    return pl.pallas_call(
        _adaln_kernel,
        grid=(n // tn,),
        in_specs=[
            pl.BlockSpec((r, d), lambda j: (0, 0)),
            pl.BlockSpec((d, tn), lambda j: (0, j)),
            pl.BlockSpec((1, tn), lambda j: (0, j)),
        ],
        out_specs=pl.BlockSpec((r, tn), lambda j: (0, j)),
        out_shape=jax.ShapeDtypeStruct((r, n), F32),
        compiler_params=_params(("arbitrary",), 40),
        name="adaln",
    )(cvecs, w_ada, b_ada.reshape(1, n))


def _gelu_tanh(x):
    a = -2.0 * math.sqrt(2.0 / math.pi) * LOG2E
    return x / (1.0 + jnp.exp2(x * (a + (a * 0.044715) * (x * x))))


def _norm_mod_matmul_kernel(x_hbm, nw_ref, sc_ref, sh_ref, w_ref, cs_ref, *rest,
                            rows_chunk, epilogue, gelu_from):
    outs, (x_buf, h_ref, x_sem) = rest[:-3], rest[-3:]
    tm = x_buf.shape[0]
    bb = pl.program_id(0)
    i = pl.program_id(1)
    j = pl.program_id(2)
    n_i = pl.num_programs(1)
    n_tiles = pl.num_programs(0) * n_i

    def x_copy(tile):
        return pltpu.make_async_copy(
            x_hbm.at[tile // n_i, pl.ds((tile % n_i) * tm, tm), :], x_buf, x_sem)

    @pl.when(j == 0)
    def _():
        tile = bb * n_i + i

        @pl.when(tile == 0)
        def _():
            x_copy(tile).start()

        x_copy(tile).wait()

        def body(c, carry):
            r = pl.multiple_of(c * rows_chunk, rows_chunk)
            x = x_buf[pl.ds(r, rows_chunk), :]
            ms = jnp.mean(x * x, axis=-1, keepdims=True)
            gain = nw_ref[...] * (1.0 + sc_ref[...])
            h = x * lax.rsqrt(ms + NORM_EPS) * gain + sh_ref[...]
            h_ref[pl.ds(r, rows_chunk), :] = h.astype(BF16)
            return carry
        lax.fori_loop(0, tm // rows_chunk, body, 0, unroll=NORM_UNROLL)

        @pl.when(tile + 1 < n_tiles)
        def _():
            x_copy(tile + 1).start()

    def matmul():
        return jnp.dot(h_ref[...], w_ref[...], preferred_element_type=F32)

    if epilogue == "plain":
        outs[0][...] = matmul().astype(BF16)
    else:
        assert epilogue == "heads_or_gelu"
        heads_ref, gated_ref = outs

        @pl.when(j < gelu_from)
        def _():
            val = (matmul() * cs_ref[...]).astype(BF16)
            for hh in range(heads_ref.shape[0]):
                heads_ref[hh] = val[:, hh * HEAD_DIM:(hh + 1) * HEAD_DIM]

        @pl.when(j >= gelu_from)
        def _():
            gated_ref[...] = _gelu_tanh(matmul()).astype(BF16)


def _norm_mod_matmul(x, norm_w, scale, shift, w, *, epilogue="plain", col_scale=None, gelu_cols=0,
                     n_lo=0, n_hi=None, tm=1024, tn=1024, vmem_mib=56):
    b, s, k = x.shape
    n_hi = w.shape[1] if n_hi is None else n_hi
    n = n_hi - n_lo
    tm = _tile(s, tm, align=8)
    tn = _tile(n, tn, divides=(n_lo, n - gelu_cols))
    j0 = n_lo // tn
    gelu_from = (n - gelu_cols) // tn
    if col_scale is None:
        col_scale = jnp.ones((n,), F32)
    per_batch = scale.shape[0] == b and b > 1
    mod_map = (lambda bb, i, j: (bb, 0, 0)) if per_batch else (lambda bb, i, j: (0, 0, 0))
    if epilogue == "heads_or_gelu":
        hpb = tn // HEAD_DIM
        out_specs = [
            pl.BlockSpec((None, hpb, tm, HEAD_DIM),
                         lambda bb, i, j: (bb, jnp.minimum(j, gelu_from - 1), i, 0)),
            pl.BlockSpec((None, tm, tn), lambda bb, i, j: (bb, i, jnp.maximum(j - gelu_from, 0))),
        ]
        out_shape = [jax.ShapeDtypeStruct((b, (n - gelu_cols) // HEAD_DIM, s, HEAD_DIM), BF16),
                     jax.ShapeDtypeStruct((b, s, gelu_cols), BF16)]
    else:
        out_specs = [pl.BlockSpec((None, tm, tn), lambda bb, i, j: (bb, i, j))]
        out_shape = [jax.ShapeDtypeStruct((b, s, n), BF16)]
    out = pl.pallas_call(
        functools.partial(_norm_mod_matmul_kernel, rows_chunk=min(NORM_ROWS, tm), epilogue=epilogue,
                          gelu_from=gelu_from),
        grid=(b, s // tm, n // tn),
        in_specs=[
            pl.BlockSpec(memory_space=pl.ANY),
            pl.BlockSpec((1, k), lambda bb, i, j: (0, 0)),
            pl.BlockSpec((None, 1, k), mod_map),
            pl.BlockSpec((None, 1, k), mod_map),
            pl.BlockSpec((k, tn), lambda bb, i, j: (0, j0 + j)),
            pl.BlockSpec((1, tn), lambda bb, i, j: (0, j)),
        ],
        out_specs=out_specs,
        out_shape=out_shape,
        scratch_shapes=[pltpu.VMEM((tm, k), F32), pltpu.VMEM((tm, k), BF16),
                        pltpu.SemaphoreType.DMA(())],
        compiler_params=_params(("arbitrary", "arbitrary", "arbitrary"), vmem_mib),
        name="norm_mod_matmul_" + epilogue,
    )(x, norm_w.reshape(1, k), scale, shift, w, col_scale.reshape(1, n))
    return out if len(out) > 1 else out[0]


def _ff1_kernel(x_hbm, nw_ref, sc_ref, sh_ref, w_ref, o_ref, x_ring, h_even, h_odd, sems, *,
                n_i, n_j, n_tiles, rows_chunk, per_batch):
    tm = h_even.shape[0]
    rs = tm // n_j
    total = n_tiles * n_j
    tile = pl.program_id(0) * n_i + pl.program_id(1)
    j = pl.program_id(2)

    def chunk_copy(g):
        t, c = g // n_j, g % n_j
        return pltpu.make_async_copy(
            x_hbm.at[t // n_i, pl.ds((t % n_i) * tm + c * rs, rs), :],
            x_ring.at[g % FF1_RING], sems.at[g % FF1_RING])

    def norm_chunk(g, dst):
        t, c = g // n_j, g % n_j
        bm = jnp.minimum(t, n_tiles - 1) // n_i if per_batch else 0
        gain = nw_ref[...] * (1.0 + sc_ref[bm])
        shift = sh_ref[bm]
        for sub in range(rs // rows_chunk):
            x = x_ring[g % FF1_RING, sub * rows_chunk:(sub + 1) * rows_chunk, :]
            ms = jnp.mean(x * x, axis=-1, keepdims=True)
            h = x * lax.rsqrt(ms + NORM_EPS) * gain + shift
            r = pl.multiple_of(c * rs + sub * rows_chunk, rows_chunk)
            dst[pl.ds(r, rows_chunk), :] = h.astype(BF16)

    @pl.when((tile == 0) & (j == 0))
    def _():
        for g in range(min(FF1_RING, total)):
            chunk_copy(g).start()
        for g in range(n_j):
            chunk_copy(g).wait()
            norm_chunk(g, h_even)
            if g + FF1_RING < total:
                chunk_copy(g + FF1_RING).start()

    g = (tile + 1) * n_j + j

    @pl.when(g < total)
    def _():
        chunk_copy(g).wait()

    def step(h_cur, h_next):
        norm_chunk(g, h_next)
        acc = jnp.dot(h_cur[...], w_ref[...], preferred_element_type=F32)
        o_ref[...] = jnp.square(jnp.maximum(acc, 0.0)).astype(o_ref.dtype)

    @pl.when(tile % 2 == 0)
    def _():
        step(h_even, h_odd)

    @pl.when(tile % 2 == 1)
    def _():
        step(h_odd, h_even)

    @pl.when(g + FF1_RING < total)
    def _():
        chunk_copy(g + FF1_RING).start()


def _ff1(x, norm_w, scale, shift, w, *, tm=1024, tn=1024):
    b, s, k = x.shape
    n = w.shape[1]
    tm = _tile(s, tm, align=8)
    tn = _tile(n, tn)
    n_i, n_j = s // tm, n // tn
    rs = tm // n_j
    rows_chunk = min(NORM_ROWS, rs)
    assert tm % n_j == 0 and rs % rows_chunk == 0 and n_j % FF1_RING == 0
    nb = scale.shape[0]
    whole = lambda bb, i, j: (0, 0, 0)
    return pl.pallas_call(
        functools.partial(_ff1_kernel, n_i=n_i, n_j=n_j, n_tiles=b * n_i, rows_chunk=rows_chunk,
                          per_batch=nb == b and b > 1),
        grid=(b, n_i, n_j),
        in_specs=[
            pl.BlockSpec(memory_space=pl.ANY),
            pl.BlockSpec((1, k), lambda bb, i, j: (0, 0)),
            pl.BlockSpec((nb, 1, k), whole),
            pl.BlockSpec((nb, 1, k), whole),
            pl.BlockSpec((k, tn), lambda bb, i, j: (0, j)),
        ],
        out_specs=pl.BlockSpec((None, tm, tn), lambda bb, i, j: (bb, i, j)),
        out_shape=jax.ShapeDtypeStruct((b, s, n), BF16),
        scratch_shapes=[pltpu.VMEM((FF1_RING, rs, k), F32), pltpu.VMEM((tm, k), BF16),
                        pltpu.VMEM((tm, k), BF16), pltpu.SemaphoreType.DMA((FF1_RING,))],
        compiler_params=_params(("arbitrary", "arbitrary", "arbitrary"), 56),
        name="ff1_norm_matmul_relu2",
    )(x, norm_w.reshape(1, k), scale, shift, w)


def _window_structure(rows, rblk):
    kh = min(NA_KH_MAX, rows)
    kstart = min(max(rblk * ATT_QROWS - kh // 2, 0), rows - ATT_KROWS)
    out = []
    for i in range(ATT_QROWS):
        qr = rblk * ATT_QROWS + i
        r0 = min(max(qr - kh // 2, 0), rows - kh)
        row = []
        for j in range(ATT_KROWS):
            kr = kstart + j
            row.append(kr - qr + NA_KH_MAX - 1 if r0 <= kr < r0 + kh else None)
        assert sum(e is not None for e in row) == kh
        out.append(tuple(row))
    return tuple(out)


def _block_types(rows):
    nblk = rows // ATT_QROWS
    reps = [0, min(1, nblk - 1), nblk - 1]
    structs = [_window_structure(rows, r) for r in reps]
    for r in range(nblk):
        t = 0 if r == 0 else (2 if r == nblk - 1 else 1)
        assert _window_structure(rows, r) == structs[t]
    return structs


def _attn_kernel(rpb_ref, q_ref, k_ref, v_ref, kc_ref, vc_ref, *rest, rows, n_cast):
    cast_in, o_ref, cast_out = rest[:n_cast], rest[n_cast], rest[n_cast + 1:2 * n_cast + 1]
    bias_ref, s_bufs, p_bufs, l_bufs = (rest[2 * n_cast + 1], rest[2 * n_cast + 2:2 * n_cast + 4],
                                        rest[2 * n_cast + 4:2 * n_cast + 6], rest[2 * n_cast + 6:])
    for src, dst in zip(cast_in, cast_out):
        dst[...] = src[...].astype(dst.dtype)

    w = GRID_W
    n_ro = 2 * NA_KH_MAX - 1
    n_co = 2 * NA_KW - 1
    nblk = rows // ATT_QROWS
    kh = min(NA_KH_MAX, rows)
    tq = ATT_QROWS * w
    tk = ATT_KROWS * w
    h = pl.program_id(0)
    b = pl.program_id(1)

    @pl.when(b == 0)
    def _():
        qc = lax.broadcasted_iota(jnp.int32, (w, 2 * w), 0)
        lane = lax.broadcasted_iota(jnp.int32, (w, 2 * w), 1)
        kc = lane % w
        cs = jnp.clip(qc - NA_KW // 2, 0, w - NA_KW)
        col_ok = (kc >= cs) & (kc < cs + NA_KW)
        diff = kc - qc + (NA_KW - 1)
        hit = [(diff == d) & col_ok for d in range(n_co)]
        base = h * (n_ro * n_co)
        neg = jnp.full((w, 2 * w), MASK_VALUE, F32)
        tables = []
        for ro in range(n_ro):
            acc = neg
            for d in range(n_co):
                acc = jnp.where(hit[d], rpb_ref[base + ro * n_co + d] * LOG2E, acc)
            tables.append(acc)
        left = lane < w
        for t, struct in enumerate(_block_types(rows)):
            for i in range(ATT_QROWS):
                for jp in range(ATT_KROWS // 2):
                    ro_l, ro_r = struct[i][2 * jp], struct[i][2 * jp + 1]
                    blk_l = neg if ro_l is None else tables[ro_l]
                    blk_r = neg if ro_r is None else tables[ro_r]
                    blk = blk_l if ro_l == ro_r else jnp.where(left, blk_l, blk_r)
                    bias_ref[t, i * w:(i + 1) * w, jp * 2 * w:(jp + 1) * 2 * w] = blk

    nt = (((1,), (1,)), ((), ()))
    last = nblk - 1

    def key_start(r):
        kstart = jnp.clip(r * ATT_QROWS - kh // 2, 0, rows - ATT_KROWS)
        return pl.multiple_of(kstart * w, w)

    def scores(r, slot):
        r = jnp.minimum(r, last)
        q = q_ref[pl.ds(pl.multiple_of(r * tq, tq), tq), :]
        kw = k_ref[pl.ds(key_start(r), tk), :]
        t = jnp.where(r == 0, 0, jnp.where(r == last, 2, 1))
        s_bufs[slot][:, 0:tk] = lax.dot_general(q, kw, nt, preferred_element_type=F32) + bias_ref[t]
        s_bufs[slot][:, tk:] = lax.dot_general(q, kc_ref[...], nt, preferred_element_type=F32)

    def softmax(slot):
        s = s_bufs[slot][...]
        p = jnp.exp2(s - jnp.max(s, axis=-1, keepdims=True))
        l_bufs[slot][...] = 1.0 / jnp.sum(p, axis=-1, keepdims=True)
        p_bufs[slot][...] = p.astype(BF16)

    def values(r, slot):
        vw = v_ref[pl.ds(key_start(r), tk), :]
        o = (jnp.dot(p_bufs[slot][:, 0:tk], vw, preferred_element_type=F32)
             + jnp.dot(p_bufs[slot][:, tk:], vc_ref[...], preferred_element_type=F32))
        o_ref[pl.ds(pl.multiple_of(r * tq, tq), tq), :] = (o * l_bufs[slot][...]).astype(o_ref.dtype)

    scores(0, 0)
    scores(1, 1)
    softmax(0)

    def body(it, carry):
        r = 2 * it
        scores(r + 2, 0)
        softmax(1)
        values(r, 0)
        scores(r + 3, 1)
        softmax(0)
        values(r + 1, 1)
        return carry
    lax.fori_loop(0, nblk // 2, body, 0, unroll=4)


def _attention(qkv, ctx_kv, rpb, cast_weights):
    b, _, s, _ = qkv.shape
    c = ctx_kv.shape[1]
    nh, dh = NA_HEADS, HEAD_DIM
    rows = s // GRID_W
    assert s % GRID_W == 0 and rows % (2 * ATT_QROWS) == 0 and rows >= ATT_KROWS
    tq, tk = ATT_QROWS * GRID_W, ATT_KROWS * GRID_W
    cast_specs = _cast_specs(cast_weights, nh * b, lambda h, bb: (h * b + bb, 0))
    return pl.pallas_call(
        functools.partial(_attn_kernel, rows=rows, n_cast=len(cast_weights)),
        grid=(nh, b),
        in_specs=[
            pl.BlockSpec(memory_space=pltpu.SMEM),
            pl.BlockSpec((None, None, s, dh), lambda h, bb: (bb, h, 0, 0)),
            pl.BlockSpec((None, None, s, dh), lambda h, bb: (bb, nh + h, 0, 0)),
            pl.BlockSpec((None, None, s, dh), lambda h, bb: (bb, 2 * nh + h, 0, 0)),
            pl.BlockSpec((None, c, dh), lambda h, bb: (bb, 0, h)),
            pl.BlockSpec((None, c, dh), lambda h, bb: (bb, 0, nh + h)),
        ] + cast_specs,
        out_specs=[pl.BlockSpec((None, s, dh), lambda h, bb: (bb, 0, h))] + cast_specs,
        out_shape=[jax.ShapeDtypeStruct((b, s, nh * dh), BF16)]
        + [jax.ShapeDtypeStruct(wgt.shape, BF16) for wgt in cast_weights],
        scratch_shapes=[pltpu.VMEM((3, tq, tk), F32)]
        + [pltpu.VMEM((tq, tk + c), F32)] * 2 + [pltpu.VMEM((tq, tk + c), BF16)] * 2
        + [pltpu.VMEM((tq, 1), F32)] * 2,
        compiler_params=_params(("arbitrary", "arbitrary"), 56),
        name="nbr_attention",
    )(rpb.reshape(-1), qkv, qkv, qkv, ctx_kv, ctx_kv, *cast_weights)


def _mix_out_kernel(gated_hbm, ona_hbm, x_ref, gate_ref, wout_ref, ws_ref, bs_ref, snw_ref,
                    gna_ref, gsg_ref, o_ref, u_ring, g_ring, ona_ring, m_even, m_odd, sems, *,
                    n_i, n_j, n_tiles):
    tm = m_even.shape[0]
    d_sgu = u_ring.shape[2]
    d_na = ona_ring.shape[2]
    gd = d_sgu // SGU_GROUPS
    total = n_tiles * n_j
    tile = pl.program_id(0) * n_i + pl.program_id(1)
    j = pl.program_id(2)

    def copies(g):
        t, c, slot = g // n_j, g % n_j, g % MIX_RING
        bb, rows = t // n_i, pl.ds((t % n_i) * tm + c * SGU_CHUNK, SGU_CHUNK)
        return (pltpu.make_async_copy(gated_hbm.at[bb, rows, pl.ds(0, d_sgu)], u_ring.at[slot],
                                      sems.at[0, slot]),
                pltpu.make_async_copy(gated_hbm.at[bb, rows, pl.ds(d_sgu, d_sgu)], g_ring.at[slot],
                                      sems.at[1, slot]),
                pltpu.make_async_copy(ona_hbm.at[bb, rows, :], ona_ring.at[slot], sems.at[2, slot]))

    def mix_chunk(g, dst):
        slot = g % MIX_RING
        rs = pl.ds(pl.multiple_of((g % n_j) * SGU_CHUNK, SGU_CHUNK), SGU_CHUNK)
        gg = g_ring[slot].astype(F32)
        ms = jnp.mean(gg * gg, axis=-1, keepdims=True)
        gn = (gg * lax.rsqrt(ms + NORM_EPS) * snw_ref[...]).astype(BF16)
        parts = []
        ssq = jnp.zeros((SGU_CHUNK, 1), F32)
        for grp in range(SGU_GROUPS):
            cols = slice(grp * gd, (grp + 1) * gd)
            mixed = jnp.dot(ws_ref[grp], gn[:, cols], preferred_element_type=F32) + bs_ref[grp]
            og = u_ring[slot, :, cols].astype(F32) * mixed
            ssq = ssq + jnp.sum(og * og, axis=-1, keepdims=True)
            parts.append(og)
        inv = lax.rsqrt(ssq / d_sgu + NORM_EPS)
        for grp in range(SGU_GROUPS):
            cols = slice(grp * gd, (grp + 1) * gd)
            dst[rs, d_na + grp * gd:d_na + (grp + 1) * gd] = (
                parts[grp] * inv * gsg_ref[:, cols]).astype(BF16)
        on = ona_ring[slot].astype(F32)
        ms = jnp.mean(on * on, axis=-1, keepdims=True)
        dst[rs, 0:d_na] = (on * lax.rsqrt(ms + NORM_EPS) * gna_ref[...]).astype(BF16)

    @pl.when((tile == 0) & (j == 0))
    def _():
        for g in range(min(MIX_RING, total)):
            for cp in copies(g):
                cp.start()
        for g in range(n_j):
            for cp in copies(g):
                cp.wait()
            mix_chunk(g, m_even)
            if g + MIX_RING < total:
                for cp in copies(g + MIX_RING):
                    cp.start()

    g = (tile + 1) * n_j + j

    @pl.when(g < total)
    def _():
        for cp in copies(g):
            cp.wait()

    def step(m_cur, m_next):
        mix_chunk(g, m_next)
        acc = jnp.dot(m_cur[...], wout_ref[...], preferred_element_type=F32)
        o_ref[...] = x_ref[...] + gate_ref[...] * acc

    @pl.when(tile % 2 == 0)
    def _():
        step(m_even, m_odd)

    @pl.when(tile % 2 == 1)
    def _():
        step(m_odd, m_even)

    @pl.when(g + MIX_RING < total)
    def _():
        for cp in copies(g + MIX_RING):
            cp.start()


def _mix_out(gated, o_na, x, gate, w_out, sgu_w, sgu_b, sgu_norm_w, gn_na, gn_sgu, *, tm=1024):
    b, s, d = x.shape
    d_na = o_na.shape[2]
    d_sgu = d - d_na
    tm = _tile(s, tm)
    n_i, n_j = s // tm, tm // SGU_CHUNK
    tn = d // n_j
    assert (tm % SGU_CHUNK == 0 and d % n_j == 0 and tn % 128 == 0 and n_j % MIX_RING == 0
            and gated.shape[2] == 2 * d_sgu)
    const2 = lambda bb, i, j: (0, 0)
    const3 = lambda bb, i, j: (0, 0, 0)
    return pl.pallas_call(
        functools.partial(_mix_out_kernel, n_i=n_i, n_j=n_j, n_tiles=b * n_i),
        grid=(b, n_i, n_j),
        in_specs=[
            pl.BlockSpec(memory_space=pl.ANY),
            pl.BlockSpec(memory_space=pl.ANY),
            pl.BlockSpec((None, tm, tn), lambda bb, i, j: (bb, i, j)),
            pl.BlockSpec((None, 1, tn), lambda bb, i, j: (bb, 0, j)),
            pl.BlockSpec((d, tn), lambda bb, i, j: (0, j)),
            pl.BlockSpec((SGU_GROUPS, SGU_CHUNK, SGU_CHUNK), const3),
            pl.BlockSpec((SGU_GROUPS, SGU_CHUNK, 1), const3),
            pl.BlockSpec((1, d_sgu), const2),
            pl.BlockSpec((1, d_na), const2),
            pl.BlockSpec((1, d_sgu), const2),
        ],
        out_specs=pl.BlockSpec((None, tm, tn), lambda bb, i, j: (bb, i, j)),
        out_shape=jax.ShapeDtypeStruct((b, s, d), F32),
        scratch_shapes=[pltpu.VMEM((MIX_RING, SGU_CHUNK, d_sgu), BF16),
                        pltpu.VMEM((MIX_RING, SGU_CHUNK, d_sgu), BF16),
                        pltpu.VMEM((MIX_RING, SGU_CHUNK, d_na), BF16),
                        pltpu.VMEM((tm, d), BF16), pltpu.VMEM((tm, d), BF16),
                        pltpu.SemaphoreType.DMA((3, MIX_RING))],
        compiler_params=_params(("arbitrary", "arbitrary", "arbitrary"), 48),
        name="mix_out",
    )(gated, o_na, x, gate, w_out, sgu_w.astype(BF16),
      sgu_b.reshape(SGU_GROUPS, SGU_CHUNK, 1), sgu_norm_w.reshape(1, d_sgu),
      gn_na.reshape(1, d_na), gn_sgu.reshape(1, d_sgu))


def _ff2_norm_kernel(a_ref, w_ref, x_ref, g_ref, fw_ref, o_hbm, acc_ref, row_buf, sems, *, rows_chunk):
    n_j, tm, tn = row_buf.shape
    group = NORM_UNROLL * rows_chunk
    n_groups = tm // group
    n_slots = n_groups // n_j
    bb = pl.program_id(0)
    i = pl.program_id(1)
    j = pl.program_id(2)
    kk = pl.program_id(3)
    last_k = pl.num_programs(3) - 1

    def stage_rows(grp, jj, sub=0, rows=None):
        start = ((grp % n_slots) * n_j + jj) * group + sub * rows_chunk
        return pl.ds(pl.multiple_of(start, rows_chunk), group if rows is None else rows)

    def stage_copy(grp, jj):
        return pltpu.make_async_copy(
            acc_ref.at[stage_rows(grp, jj), :],
            o_hbm.at[bb, pl.ds(i * tm + grp * group, group), pl.ds(jj * tn, tn)],
            sems.at[grp % n_slots, jj])

    def partial_product():
        return jnp.dot(a_ref[...], w_ref[...], preferred_element_type=F32)

    @pl.when(kk == 0)
    def _():
        acc_ref[...] = partial_product()

    @pl.when((kk > 0) & (kk < last_k))
    def _():
        acc_ref[...] += partial_product()

    @pl.when(kk == last_k)
    def _():
        row_buf[j] = x_ref[...] + g_ref[...] * (acc_ref[...] + partial_product())

    @pl.when((kk == last_k) & (j == n_j - 1))
    def _():
        def body(grp, carry):
            @pl.when(grp >= n_slots)
            def _():
                for jj in range(n_j):
                    stage_copy(grp - n_slots, jj).wait()

            base = pl.multiple_of(grp * group, group)
            for sub in range(NORM_UNROLL):
                rows = pl.ds(base + sub * rows_chunk, rows_chunk)
                parts = [row_buf[jj, rows, :] for jj in range(n_j)]
                ssq = sum(jnp.sum(p * p, axis=-1, keepdims=True) for p in parts)
                inv = lax.rsqrt(ssq / (n_j * tn) + NORM_EPS)
                for jj in range(n_j):
                    acc_ref[stage_rows(grp, jj, sub, rows_chunk), :] = (
                        parts[jj] * inv * fw_ref[:, jj * tn:(jj + 1) * tn])
            for jj in range(n_j):
                stage_copy(grp, jj).start()
            return carry
        lax.fori_loop(0, n_groups, body, 0)
        for grp in range(n_groups - n_slots, n_groups):
            for jj in range(n_j):
                stage_copy(grp, jj).wait()


def _ff2_norm(a, w, x, gate, final_w, *, tm=1024, tn=2048, tk=1024):
    b, s, k = a.shape
    n = w.shape[1]
    tm, tn, tk = _tile(s, tm), _tile(n, tn), _tile(k, tk)
    n_j = n // tn
    n_groups = tm // (NORM_UNROLL * NORM_ROWS)
    assert k // tk >= 2
    assert tm % (NORM_UNROLL * NORM_ROWS) == 0 and n_groups % n_j == 0 and n_groups // n_j >= 1
    return pl.pallas_call(
        functools.partial(_ff2_norm_kernel, rows_chunk=NORM_ROWS),
        grid=(b, s // tm, n // tn, k // tk),
        in_specs=[
            pl.BlockSpec((None, tm, tk), lambda bb, i, j, kk: (bb, i, kk)),
            pl.BlockSpec((tk, tn), lambda bb, i, j, kk: (kk, j)),
            pl.BlockSpec((None, tm, tn), lambda bb, i, j, kk: (bb, i, j)),
            pl.BlockSpec((None, 1, tn), lambda bb, i, j, kk: (bb, 0, j)),
            pl.BlockSpec((1, n), lambda bb, i, j, kk: (0, 0)),
        ],
        out_specs=pl.BlockSpec(memory_space=pl.ANY),
        out_shape=jax.ShapeDtypeStruct((b, s, n), F32),
        scratch_shapes=[pltpu.VMEM((tm, tn), F32), pltpu.VMEM((n_j, tm, tn), F32),
                        pltpu.SemaphoreType.DMA((n_groups // n_j, n_j))],
        compiler_params=_params(("arbitrary", "arbitrary", "arbitrary", "arbitrary"), 60),
        name="ff2_residual_norm",
    )(a, w, x, gate, final_w.reshape(1, n))


def kernel(x, c, ctx, c_ctx, w_ada, b_ada, norm1_w, w_in, rpb, sgu_norm_w, sgu_w, sgu_b,
           grp_norm_na, grp_norm_sgu, w_out, norm2_w, w_ff1, w_ff2, final_norm_w):
    b, s, d = x.shape
    n_ctx = ctx.shape[1]
    d_na = NA_HEADS * HEAD_DIM
    d_sgu = d - d_na
    depth = w_ada.shape[0]
    assert depth == 1

    n_rows = -(-(b + 1) // 8) * 8
    cvecs = jnp.zeros((n_rows, d), F32).at[:b].set(c).at[b].set(c_ctx)
    mod = _adaln(cvecs, w_ada[0], b_ada[0]).reshape(n_rows, N_MOD, d)
    lat = lambda i: mod[:b, i].reshape(b, 1, d)
    cx = lambda i: mod[b, i].reshape(1, 1, d)

    w_in_b = w_in[0].astype(BF16)
    qkv_scale = jnp.concatenate([jnp.full((d_na,), HEAD_DIM ** -0.5 * LOG2E, F32),
                                 jnp.ones((2 * d_na + 2 * d_sgu,), F32)])
    qkv, gated = _norm_mod_matmul(x, norm1_w[0], lat(1), lat(0), w_in_b, epilogue="heads_or_gelu",
                                  col_scale=qkv_scale, gelu_cols=2 * d_sgu, vmem_mib=60)
    ctx_kv = _norm_mod_matmul(ctx.reshape(1, b * n_ctx, d), norm1_w[0], cx(1), cx(0), w_in_b,
                              n_lo=d_na, n_hi=3 * d_na).reshape(b, n_ctx, 2 * d_na)
    o_na, w_out_b, w_ff1_b, w_ff2_b = _attention(qkv, ctx_kv, rpb[0], [w_out[0], w_ff1[0], w_ff2[0]])
    x1 = _mix_out(gated, o_na, x, lat(2), w_out_b, sgu_w[0], sgu_b[0], sgu_norm_w[0],
                  grp_norm_na[0], grp_norm_sgu[0])
    hidden = _ff1(x1, norm2_w[0], lat(4), lat(3), w_ff1_b)
    return _ff2_norm(hidden, w_ff2_b, x1, lat(5), final_norm_w)
```

```python
import functools
import math

import jax
import jax.numpy as jnp
from jax import lax
from jax.experimental import pallas as pl
from jax.experimental.pallas import tpu as pltpu

F32 = jnp.float32
BF16 = jnp.bfloat16

GRID_W = 64
NA_HEADS = 16
HEAD_DIM = 128
NA_KH_MAX = 8
NA_KW = 16
SGU_GROUPS = 4
SGU_CHUNK = 128
N_MOD = 6
NORM_EPS = 1e-6
MASK_VALUE = -1e30
LOG2E = math.log2(math.e)

ATT_QROWS = 4
ATT_KROWS = 12
NORM_ROWS = 16
NORM_UNROLL = 4
FF1_RING = 4
MIX_RING = 4
FF2_STAGE_SLOTS = 8

V7X_VMEM_BYTES = 64 << 20


def _params(semantics, vmem_mib):
    assert (vmem_mib << 20) < V7X_VMEM_BYTES
    return pltpu.CompilerParams(dimension_semantics=semantics, vmem_limit_bytes=vmem_mib << 20)


def _tile(n, pref, align=128, divides=()):
    for t in range(min(pref, n) // align * align, 0, -align):
        if n % t == 0 and all(v % t == 0 for v in divides):
            return t
    assert not any(divides)
    return n


def _cast_specs(weights, steps, step_index_map):
    specs = []
    for wgt in weights:
        slab = wgt.shape[0] // steps
        assert wgt.shape[0] % steps == 0 and slab % 16 == 0
        specs.append(pl.BlockSpec((slab, wgt.shape[1]), step_index_map))
    return specs


def _adaln_kernel(c_ref, w_ref, b_ref, o_ref):
    c = c_ref[...]
    a = (c * jax.nn.sigmoid(c)).astype(BF16)
    o_ref[...] = jnp.dot(a, w_ref[...].astype(BF16), preferred_element_type=F32) + b_ref[...]


def _adaln(cvecs, w_ada, b_ada):
    r, d = cvecs.shape
    n = w_ada.shape[1]
    tn = _tile(n, 512)
    return pl.pallas_call(
        _adaln_kernel,
        grid=(n // tn,),
        in_specs=[
            pl.BlockSpec((r, d), lambda j: (0, 0)),
            pl.BlockSpec((d, tn), lambda j: (0, j)),
            pl.BlockSpec((1, tn), lambda j: (0, j)),
        ],
        out_specs=pl.BlockSpec((r, tn), lambda j: (0, j)),
        out_shape=jax.ShapeDtypeStruct((r, n), F32),
        compiler_params=_params(("arbitrary",), 40),
        name="adaln",
    )(cvecs, w_ada, b_ada.reshape(1, n))


def _gelu_tanh(x):
    a = -2.0 * math.sqrt(2.0 / math.pi) * LOG2E
    return x / (1.0 + jnp.exp2(x * (a + (a * 0.044715) * (x * x))))


def _norm_mod_matmul_kernel(x_hbm, nw_ref, sc_ref, sh_ref, w_ref, cs_ref, *rest,
                            rows_chunk, epilogue, gelu_from):
    outs, (x_buf, h_ref, x_sem) = rest[:-3], rest[-3:]
    tm = x_buf.shape[0]
    bb = pl.program_id(0)
    i = pl.program_id(1)
    j = pl.program_id(2)
    n_i = pl.num_programs(1)
    n_tiles = pl.num_programs(0) * n_i

    def x_copy(tile):
        return pltpu.make_async_copy(
            x_hbm.at[tile // n_i, pl.ds((tile % n_i) * tm, tm), :], x_buf, x_sem)

    @pl.when(j == 0)
    def _():
        tile = bb * n_i + i

        @pl.when(tile == 0)
        def _():
            x_copy(tile).start()

        x_copy(tile).wait()

        def body(c, carry):
            r = pl.multiple_of(c * rows_chunk, rows_chunk)
            x = x_buf[pl.ds(r, rows_chunk), :]
            ms = jnp.mean(x * x, axis=-1, keepdims=True)
            gain = nw_ref[...] * (1.0 + sc_ref[...])
            h = x * lax.rsqrt(ms + NORM_EPS) * gain + sh_ref[...]
            h_ref[pl.ds(r, rows_chunk), :] = h.astype(BF16)
            return carry
        lax.fori_loop(0, tm // rows_chunk, body, 0, unroll=NORM_UNROLL)

        @pl.when(tile + 1 < n_tiles)
        def _():
            x_copy(tile + 1).start()

    def matmul():
        return jnp.dot(h_ref[...], w_ref[...], preferred_element_type=F32)

    if epilogue == "plain":
        outs[0][...] = matmul().astype(BF16)
    else:
        assert epilogue == "heads_or_gelu"
        heads_ref, gated_ref = outs

        @pl.when(j < gelu_from)
        def _():
            val = (matmul() * cs_ref[...]).astype(BF16)
            for hh in range(heads_ref.shape[0]):
                heads_ref[hh] = val[:, hh * HEAD_DIM:(hh + 1) * HEAD_DIM]

        @pl.when(j >= gelu_from)
        def _():
            gated_ref[...] = _gelu_tanh(matmul()).astype(BF16)


def _norm_mod_matmul(x, norm_w, scale, shift, w, *, epilogue="plain", col_scale=None, gelu_cols=0,
                     n_lo=0, n_hi=None, tm=1024, tn=1024, vmem_mib=56):
    b, s, k = x.shape
    n_hi = w.shape[1] if n_hi is None else n_hi
    n = n_hi - n_lo
    tm = _tile(s, tm, align=8)
    tn = _tile(n, tn, divides=(n_lo, n - gelu_cols))
    j0 = n_lo // tn
    gelu_from = (n - gelu_cols) // tn
    if col_scale is None:
        col_scale = jnp.ones((n,), F32)
    per_batch = scale.shape[0] == b and b > 1
    mod_map = (lambda bb, i, j: (bb, 0, 0)) if per_batch else (lambda bb, i, j: (0, 0, 0))
    if epilogue == "heads_or_gelu":
        hpb = tn // HEAD_DIM
        out_specs = [
            pl.BlockSpec((None, hpb, tm, HEAD_DIM),
                         lambda bb, i, j: (bb, jnp.minimum(j, gelu_from - 1), i, 0)),
            pl.BlockSpec((None, tm, tn), lambda bb, i, j: (bb, i, jnp.maximum(j - gelu_from, 0))),
        ]
        out_shape = [jax.ShapeDtypeStruct((b, (n - gelu_cols) // HEAD_DIM, s, HEAD_DIM), BF16),
                     jax.ShapeDtypeStruct((b, s, gelu_cols), BF16)]
    else:
        out_specs = [pl.BlockSpec((None, tm, tn), lambda bb, i, j: (bb, i, j))]
        out_shape = [jax.ShapeDtypeStruct((b, s, n), BF16)]
    out = pl.pallas_call(
        functools.partial(_norm_mod_matmul_kernel, rows_chunk=min(NORM_ROWS, tm), epilogue=epilogue,
                          gelu_from=gelu_from),
        grid=(b, s // tm, n // tn),
        in_specs=[
            pl.BlockSpec(memory_space=pl.ANY),
            pl.BlockSpec((1, k), lambda bb, i, j: (0, 0)),
            pl.BlockSpec((None, 1, k), mod_map),
            pl.BlockSpec((None, 1, k), mod_map),
            pl.BlockSpec((k, tn), lambda bb, i, j: (0, j0 + j)),
            pl.BlockSpec((1, tn), lambda bb, i, j: (0, j)),
        ],
        out_specs=out_specs,
        out_shape=out_shape,
        scratch_shapes=[pltpu.VMEM((tm, k), F32), pltpu.VMEM((tm, k), BF16),
                        pltpu.SemaphoreType.DMA(())],
        compiler_params=_params(("arbitrary", "arbitrary", "arbitrary"), vmem_mib),
        name="norm_mod_matmul_" + epilogue,
    )(x, norm_w.reshape(1, k), scale, shift, w, col_scale.reshape(1, n))
    return out if len(out) > 1 else out[0]


def _ff1_kernel(x_hbm, nw_ref, sc_ref, sh_ref, w_ref, o_ref, x_ring, h_even, h_odd, sems, *,
                n_i, n_j, n_tiles, rows_chunk, per_batch):
    tm = h_even.shape[0]
    rs = tm // n_j
    total = n_tiles * n_j
    tile = pl.program_id(0) * n_i + pl.program_id(1)
    j = pl.program_id(2)

    def chunk_copy(g):
        t, c = g // n_j, g % n_j
        return pltpu.make_async_copy(
            x_hbm.at[t // n_i, pl.ds((t % n_i) * tm + c * rs, rs), :],
            x_ring.at[g % FF1_RING], sems.at[g % FF1_RING])

    def norm_chunk(g, dst):
        t, c = g // n_j, g % n_j
        bm = jnp.minimum(t, n_tiles - 1) // n_i if per_batch else 0
        gain = nw_ref[...] * (1.0 + sc_ref[bm])
        shift = sh_ref[bm]
        for sub in range(rs // rows_chunk):
            x = x_ring[g % FF1_RING, sub * rows_chunk:(sub + 1) * rows_chunk, :]
            ms = jnp.mean(x * x, axis=-1, keepdims=True)
            h = x * lax.rsqrt(ms + NORM_EPS) * gain + shift
            r = pl.multiple_of(c * rs + sub * rows_chunk, rows_chunk)
            dst[pl.ds(r, rows_chunk), :] = h.astype(BF16)

    @pl.when((tile == 0) & (j == 0))
    def _():
        for g in range(min(FF1_RING, total)):
            chunk_copy(g).start()
        for g in range(n_j):
            chunk_copy(g).wait()
            norm_chunk(g, h_even)
            if g + FF1_RING < total:
                chunk_copy(g + FF1_RING).start()

    g = (tile + 1) * n_j + j

    @pl.when(g < total)
    def _():
        chunk_copy(g).wait()

    def step(h_cur, h_next):
        norm_chunk(g, h_next)
        acc = jnp.dot(h_cur[...], w_ref[...], preferred_element_type=F32)
        o_ref[...] = jnp.square(jnp.maximum(acc, 0.0)).astype(o_ref.dtype)

    @pl.when(tile % 2 == 0)
    def _():
        step(h_even, h_odd)

    @pl.when(tile % 2 == 1)
    def _():
        step(h_odd, h_even)

    @pl.when(g + FF1_RING < total)
    def _():
        chunk_copy(g + FF1_RING).start()


def _ff1(x, norm_w, scale, shift, w, *, tm=1024, tn=1024):
    b, s, k = x.shape
    n = w.shape[1]
    tm = _tile(s, tm, align=8)
    tn = _tile(n, tn)
    n_i, n_j = s // tm, n // tn
    rs = tm // n_j
    rows_chunk = min(NORM_ROWS, rs)
    assert tm % n_j == 0 and rs % rows_chunk == 0 and n_j % FF1_RING == 0
    nb = scale.shape[0]
    whole = lambda bb, i, j: (0, 0, 0)
    return pl.pallas_call(
        functools.partial(_ff1_kernel, n_i=n_i, n_j=n_j, n_tiles=b * n_i, rows_chunk=rows_chunk,
                          per_batch=nb == b and b > 1),
        grid=(b, n_i, n_j),
        in_specs=[
            pl.BlockSpec(memory_space=pl.ANY),
            pl.BlockSpec((1, k), lambda bb, i, j: (0, 0)),
            pl.BlockSpec((nb, 1, k), whole),
            pl.BlockSpec((nb, 1, k), whole),
            pl.BlockSpec((k, tn), lambda bb, i, j: (0, j)),
        ],
        out_specs=pl.BlockSpec((None, tm, tn), lambda bb, i, j: (bb, i, j)),
        out_shape=jax.ShapeDtypeStruct((b, s, n), BF16),
        scratch_shapes=[pltpu.VMEM((FF1_RING, rs, k), F32), pltpu.VMEM((tm, k), BF16),
                        pltpu.VMEM((tm, k), BF16), pltpu.SemaphoreType.DMA((FF1_RING,))],
        compiler_params=_params(("arbitrary", "arbitrary", "arbitrary"), 56),
        name="ff1_norm_matmul_relu2",
    )(x, norm_w.reshape(1, k), scale, shift, w)


def _window_structure(rows, rblk):
    kh = min(NA_KH_MAX, rows)
    kstart = min(max(rblk * ATT_QROWS - kh // 2, 0), rows - ATT_KROWS)
    out = []
    for i in range(ATT_QROWS):
        qr = rblk * ATT_QROWS + i
        r0 = min(max(qr - kh // 2, 0), rows - kh)
        row = []
        for j in range(ATT_KROWS):
            kr = kstart + j
            row.append(kr - qr + NA_KH_MAX - 1 if r0 <= kr < r0 + kh else None)
        assert sum(e is not None for e in row) == kh
        out.append(tuple(row))
    return tuple(out)


def _block_types(rows):
    nblk = rows // ATT_QROWS
    reps = [0, min(1, nblk - 1), nblk - 1]
    structs = [_window_structure(rows, r) for r in reps]
    for r in range(nblk):
        t = 0 if r == 0 else (2 if r == nblk - 1 else 1)
        assert _window_structure(rows, r) == structs[t]
    return structs


def _attn_kernel(rpb_ref, q_ref, k_ref, v_ref, kc_ref, vc_ref, *rest, rows, n_cast):
    cast_in, o_ref, cast_out = rest[:n_cast], rest[n_cast], rest[n_cast + 1:2 * n_cast + 1]
    bias_ref, s_bufs, p_bufs, l_bufs = (rest[2 * n_cast + 1], rest[2 * n_cast + 2:2 * n_cast + 4],
                                        rest[2 * n_cast + 4:2 * n_cast + 6], rest[2 * n_cast + 6:])
    for src, dst in zip(cast_in, cast_out):
        dst[...] = src[...].astype(dst.dtype)

    w = GRID_W
    n_ro = 2 * NA_KH_MAX - 1
    n_co = 2 * NA_KW - 1
    nblk = rows // ATT_QROWS
    kh = min(NA_KH_MAX, rows)
    tq = ATT_QROWS * w
    tk = ATT_KROWS * w
    h = pl.program_id(0)
    b = pl.program_id(1)

    @pl.when(b == 0)
    def _():
        qc = lax.broadcasted_iota(jnp.int32, (w, 2 * w), 0)
        lane = lax.broadcasted_iota(jnp.int32, (w, 2 * w), 1)
        kc = lane % w
        cs = jnp.clip(qc - NA_KW // 2, 0, w - NA_KW)
        col_ok = (kc >= cs) & (kc < cs + NA_KW)
        diff = kc - qc + (NA_KW - 1)
        hit = [(diff == d) & col_ok for d in range(n_co)]
        base = h * (n_ro * n_co)
        neg = jnp.full((w, 2 * w), MASK_VALUE, F32)
        tables = []
        for ro in range(n_ro):
            acc = neg
            for d in range(n_co):
                acc = jnp.where(hit[d], rpb_ref[base + ro * n_co + d] * LOG2E, acc)
            tables.append(acc)
        left = lane < w
        for t, struct in enumerate(_block_types(rows)):
            for i in range(ATT_QROWS):
                for jp in range(ATT_KROWS // 2):
                    ro_l, ro_r = struct[i][2 * jp], struct[i][2 * jp + 1]
                    blk_l = neg if ro_l is None else tables[ro_l]
                    blk_r = neg if ro_r is None else tables[ro_r]
                    blk = blk_l if ro_l == ro_r else jnp.where(left, blk_l, blk_r)
                    bias_ref[t, i * w:(i + 1) * w, jp * 2 * w:(jp + 1) * 2 * w] = blk

    nt = (((1,), (1,)), ((), ()))
    last = nblk - 1

    def key_start(r):
        kstart = jnp.clip(r * ATT_QROWS - kh // 2, 0, rows - ATT_KROWS)
        return pl.multiple_of(kstart * w, w)

    def scores(r, slot):
        r = jnp.minimum(r, last)
        q = q_ref[pl.ds(pl.multiple_of(r * tq, tq), tq), :]
        kw = k_ref[pl.ds(key_start(r), tk), :]
        t = jnp.where(r == 0, 0, jnp.where(r == last, 2, 1))
        s_bufs[slot][:, 0:tk] = lax.dot_general(q, kw, nt, preferred_element_type=F32) + bias_ref[t]
        s_bufs[slot][:, tk:] = lax.dot_general(q, kc_ref[...], nt, preferred_element_type=F32)

    def softmax(slot):
        s = s_bufs[slot][...]
        p = jnp.exp2(s - jnp.max(s, axis=-1, keepdims=True))
        l_bufs[slot][...] = 1.0 / jnp.sum(p, axis=-1, keepdims=True)
        p_bufs[slot][...] = p.astype(BF16)

    def values(r, slot):
        vw = v_ref[pl.ds(key_start(r), tk), :]
        o = (jnp.dot(p_bufs[slot][:, 0:tk], vw, preferred_element_type=F32)
             + jnp.dot(p_bufs[slot][:, tk:], vc_ref[...], preferred_element_type=F32))
        o_ref[pl.ds(pl.multiple_of(r * tq, tq), tq), :] = (o * l_bufs[slot][...]).astype(o_ref.dtype)

    scores(0, 0)
    scores(1, 1)
    softmax(0)

    def body(it, carry):
        r = 2 * it
        scores(r + 2, 0)
        softmax(1)
        values(r, 0)
        scores(r + 3, 1)
        softmax(0)
        values(r + 1, 1)
        return carry
    lax.fori_loop(0, nblk // 2, body, 0, unroll=4)


def _attention(qkv, ctx_kv, rpb, cast_weights):
    b, _, s, _ = qkv.shape
    c = ctx_kv.shape[1]
    nh, dh = NA_HEADS, HEAD_DIM
    rows = s // GRID_W
    assert s % GRID_W == 0 and rows % (2 * ATT_QROWS) == 0 and rows >= ATT_KROWS
    tq, tk = ATT_QROWS * GRID_W, ATT_KROWS * GRID_W
    cast_specs = _cast_specs(cast_weights, nh * b, lambda h, bb: (h * b + bb, 0))
    return pl.pallas_call(
        functools.partial(_attn_kernel, rows=rows, n_cast=len(cast_weights)),
        grid=(nh, b),
        in_specs=[
            pl.BlockSpec(memory_space=pltpu.SMEM),
            pl.BlockSpec((None, None, s, dh), lambda h, bb: (bb, h, 0, 0)),
            pl.BlockSpec((None, None, s, dh), lambda h, bb: (bb, nh + h, 0, 0)),
            pl.BlockSpec((None, None, s, dh), lambda h, bb: (bb, 2 * nh + h, 0, 0)),
            pl.BlockSpec((None, c, dh), lambda h, bb: (bb, 0, h)),
            pl.BlockSpec((None, c, dh), lambda h, bb: (bb, 0, nh + h)),
        ] + cast_specs,
        out_specs=[pl.BlockSpec((None, s, dh), lambda h, bb: (bb, 0, h))] + cast_specs,
        out_shape=[jax.ShapeDtypeStruct((b, s, nh * dh), BF16)]
        + [jax.ShapeDtypeStruct(wgt.shape, BF16) for wgt in cast_weights],
        scratch_shapes=[pltpu.VMEM((3, tq, tk), F32)]
        + [pltpu.VMEM((tq, tk + c), F32)] * 2 + [pltpu.VMEM((tq, tk + c), BF16)] * 2
        + [pltpu.VMEM((tq, 1), F32)] * 2,
        compiler_params=_params(("arbitrary", "arbitrary"), 56),
        name="nbr_attention",
    )(rpb.reshape(-1), qkv, qkv, qkv, ctx_kv, ctx_kv, *cast_weights)


def _mix_out_kernel(gated_hbm, ona_hbm, x_ref, gate_ref, wout_ref, ws_ref, bs_ref, snw_ref,
                    gna_ref, gsg_ref, o_ref, u_ring, g_ring, ona_ring, m_even, m_odd, sems, *,
                    n_i, n_j, n_tiles):
    tm = m_even.shape[0]
    d_sgu = u_ring.shape[2]
    d_na = ona_ring.shape[2]
    gd = d_sgu // SGU_GROUPS
    total = n_tiles * n_j
    tile = pl.program_id(0) * n_i + pl.program_id(1)
    j = pl.program_id(2)

    def copies(g):
        t, c, slot = g // n_j, g % n_j, g % MIX_RING
        bb, rows = t // n_i, pl.ds((t % n_i) * tm + c * SGU_CHUNK, SGU_CHUNK)
        return (pltpu.make_async_copy(gated_hbm.at[bb, rows, pl.ds(0, d_sgu)], u_ring.at[slot],
                                      sems.at[0, slot]),
                pltpu.make_async_copy(gated_hbm.at[bb, rows, pl.ds(d_sgu, d_sgu)], g_ring.at[slot],
                                      sems.at[1, slot]),
                pltpu.make_async_copy(ona_hbm.at[bb, rows, :], ona_ring.at[slot], sems.at[2, slot]))

    def mix_chunk(g, dst):
        slot = g % MIX_RING
        rs = pl.ds(pl.multiple_of((g % n_j) * SGU_CHUNK, SGU_CHUNK), SGU_CHUNK)
        gg = g_ring[slot].astype(F32)
        ms = jnp.mean(gg * gg, axis=-1, keepdims=True)
        gn = (gg * lax.rsqrt(ms + NORM_EPS) * snw_ref[...]).astype(BF16)
        parts = []
        ssq = jnp.zeros((SGU_CHUNK, 1), F32)
        for grp in range(SGU_GROUPS):
            cols = slice(grp * gd, (grp + 1) * gd)
            mixed = jnp.dot(ws_ref[grp], gn[:, cols], preferred_element_type=F32) + bs_ref[grp]
            og = u_ring[slot, :, cols].astype(F32) * mixed
            ssq = ssq + jnp.sum(og * og, axis=-1, keepdims=True)
            parts.append(og)
        inv = lax.rsqrt(ssq / d_sgu + NORM_EPS)
        for grp in range(SGU_GROUPS):
            cols = slice(grp * gd, (grp + 1) * gd)
            dst[rs, d_na + grp * gd:d_na + (grp + 1) * gd] = (
                parts[grp] * inv * gsg_ref[:, cols]).astype(BF16)
        on = ona_ring[slot].astype(F32)
        ms = jnp.mean(on * on, axis=-1, keepdims=True)
        dst[rs, 0:d_na] = (on * lax.rsqrt(ms + NORM_EPS) * gna_ref[...]).astype(BF16)

    @pl.when((tile == 0) & (j == 0))
    def _():
        for g in range(min(MIX_RING, total)):
            for cp in copies(g):
                cp.start()
        for g in range(n_j):
            for cp in copies(g):
                cp.wait()
            mix_chunk(g, m_even)
            if g + MIX_RING < total:
                for cp in copies(g + MIX_RING):
                    cp.start()

    g = (tile + 1) * n_j + j

    @pl.when(g < total)
    def _():
        for cp in copies(g):
            cp.wait()

    def step(m_cur, m_next):
        mix_chunk(g, m_next)
        acc = jnp.dot(m_cur[...], wout_ref[...], preferred_element_type=F32)
        o_ref[...] = x_ref[...] + gate_ref[...] * acc

    @pl.when(tile % 2 == 0)
    def _():
        step(m_even, m_odd)

    @pl.when(tile % 2 == 1)
    def _():
        step(m_odd, m_even)

    @pl.when(g + MIX_RING < total)
    def _():
        for cp in copies(g + MIX_RING):
            cp.start()


def _mix_out(gated, o_na, x, gate, w_out, sgu_w, sgu_b, sgu_norm_w, gn_na, gn_sgu, *, tm=1024):
    b, s, d = x.shape
    d_na = o_na.shape[2]
    d_sgu = d - d_na
    tm = _tile(s, tm)
    n_i, n_j = s // tm, tm // SGU_CHUNK
    tn = d // n_j
    assert (tm % SGU_CHUNK == 0 and d % n_j == 0 and tn % 128 == 0 and n_j % MIX_RING == 0
            and gated.shape[2] == 2 * d_sgu)
    const2 = lambda bb, i, j: (0, 0)
    const3 = lambda bb, i, j: (0, 0, 0)
    return pl.pallas_call(
        functools.partial(_mix_out_kernel, n_i=n_i, n_j=n_j, n_tiles=b * n_i),
        grid=(b, n_i, n_j),
        in_specs=[
            pl.BlockSpec(memory_space=pl.ANY),
            pl.BlockSpec(memory_space=pl.ANY),
            pl.BlockSpec((None, tm, tn), lambda bb, i, j: (bb, i, j)),
            pl.BlockSpec((None, 1, tn), lambda bb, i, j: (bb, 0, j)),
            pl.BlockSpec((d, tn), lambda bb, i, j: (0, j)),
            pl.BlockSpec((SGU_GROUPS, SGU_CHUNK, SGU_CHUNK), const3),
            pl.BlockSpec((SGU_GROUPS, SGU_CHUNK, 1), const3),
            pl.BlockSpec((1, d_sgu), const2),
            pl.BlockSpec((1, d_na), const2),
            pl.BlockSpec((1, d_sgu), const2),
        ],
        out_specs=pl.BlockSpec((None, tm, tn), lambda bb, i, j: (bb, i, j)),
        out_shape=jax.ShapeDtypeStruct((b, s, d), F32),
        scratch_shapes=[pltpu.VMEM((MIX_RING, SGU_CHUNK, d_sgu), BF16),
                        pltpu.VMEM((MIX_RING, SGU_CHUNK, d_sgu), BF16),
                        pltpu.VMEM((MIX_RING, SGU_CHUNK, d_na), BF16),
                        pltpu.VMEM((tm, d), BF16), pltpu.VMEM((tm, d), BF16),
                        pltpu.SemaphoreType.DMA((3, MIX_RING))],
        compiler_params=_params(("arbitrary", "arbitrary", "arbitrary"), 48),
        name="mix_out",
    )(gated, o_na, x, gate, w_out, sgu_w.astype(BF16),
      sgu_b.reshape(SGU_GROUPS, SGU_CHUNK, 1), sgu_norm_w.reshape(1, d_sgu),
      gn_na.reshape(1, d_na), gn_sgu.reshape(1, d_sgu))


def _ff2_norm_kernel(a_ref, w_ref, x_ref, g_ref, fw_ref, o_hbm, acc_ref, row_buf, stage, sems, *,
                     rows_chunk):
    n_j, tm, tn = row_buf.shape
    n_slots, _, group, _ = stage.shape
    n_groups = tm // group
    bb = pl.program_id(0)
    i = pl.program_id(1)
    j = pl.program_id(2)
    kk = pl.program_id(3)
    last_k = pl.num_programs(3) - 1
    tile = bb * pl.num_programs(1) + i
    last_tile = pl.num_programs(0) * pl.num_programs(1) - 1

    def stage_copy(grp, jj):
        return pltpu.make_async_copy(
            stage.at[grp % n_slots, jj],
            o_hbm.at[bb, pl.ds(i * tm + grp * group, group), pl.ds(jj * tn, tn)],
            sems.at[grp % n_slots, jj])

    def partial_product():
        return jnp.dot(a_ref[...], w_ref[...], preferred_element_type=F32)

    @pl.when(kk == 0)
    def _():
        acc_ref[...] = partial_product()

    @pl.when((kk > 0) & (kk < last_k))
    def _():
        acc_ref[...] += partial_product()

    @pl.when(kk == last_k)
    def _():
        row_buf[j] = x_ref[...] + g_ref[...] * (acc_ref[...] + partial_product())

    @pl.when((kk == last_k) & (j == n_j - 1))
    def _():
        def body(grp, carry):
            @pl.when((grp >= n_slots) | (tile > 0))
            def _():
                for jj in range(n_j):
                    stage_copy(grp, jj).wait()

            base = pl.multiple_of(grp * group, group)
            for sub in range(group // rows_chunk):
                rows = pl.ds(base + sub * rows_chunk, rows_chunk)
                parts = [row_buf[jj, rows, :] for jj in range(n_j)]
                ssq = sum(jnp.sum(p * p, axis=-1, keepdims=True) for p in parts)
                inv = lax.rsqrt(ssq / (n_j * tn) + NORM_EPS)
                for jj in range(n_j):
                    stage[grp % n_slots, jj, sub * rows_chunk:(sub + 1) * rows_chunk, :] = (
                        parts[jj] * inv * fw_ref[:, jj * tn:(jj + 1) * tn])
            for jj in range(n_j):
                stage_copy(grp, jj).start()
            return carry
        lax.fori_loop(0, n_groups, body, 0)

        @pl.when(tile == last_tile)
        def _():
            for grp in range(n_groups - n_slots, n_groups):
                for jj in range(n_j):
                    stage_copy(grp, jj).wait()


def _ff2_norm(a, w, x, gate, final_w, *, tm=1024, tn=1024, tk=2048):
    b, s, k = a.shape
    n = w.shape[1]
    tm, tn, tk = _tile(s, tm), _tile(n, tn), _tile(k, tk)
    n_j = n // tn
    group = NORM_UNROLL * NORM_ROWS
    n_groups = tm // group
    assert k // tk >= 2
    assert tm % group == 0 and n_groups % FF2_STAGE_SLOTS == 0
    return pl.pallas_call(
        functools.partial(_ff2_norm_kernel, rows_chunk=NORM_ROWS),
        grid=(b, s // tm, n // tn, k // tk),
        in_specs=[
            pl.BlockSpec((None, tm, tk), lambda bb, i, j, kk: (bb, i, kk)),
            pl.BlockSpec((tk, tn), lambda bb, i, j, kk: (kk, j)),
            pl.BlockSpec((None, tm, tn), lambda bb, i, j, kk: (bb, i, j)),
            pl.BlockSpec((None, 1, tn), lambda bb, i, j, kk: (bb, 0, j)),
            pl.BlockSpec((1, n), lambda bb, i, j, kk: (0, 0)),
        ],
        out_specs=pl.BlockSpec(memory_space=pl.ANY),
        out_shape=jax.ShapeDtypeStruct((b, s, n), F32),
        scratch_shapes=[pltpu.VMEM((tm, tn), F32), pltpu.VMEM((n_j, tm, tn), F32),
                        pltpu.VMEM((FF2_STAGE_SLOTS, n_j, group, tn), F32),
                        pltpu.SemaphoreType.DMA((FF2_STAGE_SLOTS, n_j))],
        compiler_params=_params(("arbitrary", "arbitrary", "arbitrary", "arbitrary"), 60),
        name="ff2_residual_norm",
    )(a, w, x, gate, final_w.reshape(1, n))


def kernel(x, c, ctx, c_ctx, w_ada, b_ada, norm1_w, w_in, rpb, sgu_norm_w, sgu_w, sgu_b,
           grp_norm_na, grp_norm_sgu, w_out, norm2_w, w_ff1, w_ff2, final_norm_w):
    b, s, d = x.shape
    n_ctx = ctx.shape[1]
    d_na = NA_HEADS * HEAD_DIM
    d_sgu = d - d_na
    depth = w_ada.shape[0]
    assert depth == 1

    n_rows = -(-(b + 1) // 8) * 8
    cvecs = jnp.zeros((n_rows, d), F32).at[:b].set(c).at[b].set(c_ctx)
    mod = _adaln(cvecs, w_ada[0], b_ada[0]).reshape(n_rows, N_MOD, d)
    lat = lambda i: mod[:b, i].reshape(b, 1, d)
    cx = lambda i: mod[b, i].reshape(1, 1, d)

    w_in_b = w_in[0].astype(BF16)
    qkv_scale = jnp.concatenate([jnp.full((d_na,), HEAD_DIM ** -0.5 * LOG2E, F32),
                                 jnp.ones((2 * d_na + 2 * d_sgu,), F32)])
    qkv, gated = _norm_mod_matmul(x, norm1_w[0], lat(1), lat(0), w_in_b, epilogue="heads_or_gelu",
                                  col_scale=qkv_scale, gelu_cols=2 * d_sgu, vmem_mib=60)
    ctx_kv = _norm_mod_matmul(ctx.reshape(1, b * n_ctx, d), norm1_w[0], cx(1), cx(0), w_in_b,
                              n_lo=d_na, n_hi=3 * d_na).reshape(b, n_ctx, 2 * d_na)
    o_na, w_out_b, w_ff1_b, w_ff2_b = _attention(qkv, ctx_kv, rpb[0], [w_out[0], w_ff1[0], w_ff2[0]])
    x1 = _mix_out(gated, o_na, x, lat(2), w_out_b, sgu_w[0], sgu_b[0], sgu_norm_w[0],
                  grp_norm_na[0], grp_norm_sgu[0])
    hidden = _ff1(x1, norm2_w[0], lat(4), lat(3), w_ff1_b)
    return _ff2_norm(hidden, w_ff2_b, x1, lat(5), final_norm_w)
```

```python
import functools
import math

import jax
import jax.numpy as jnp
from jax import lax
from jax.experimental import pallas as pl
from jax.experimental.pallas import tpu as pltpu

F32 = jnp.float32
BF16 = jnp.bfloat16

GRID_W = 64
NA_HEADS = 16
HEAD_DIM = 128
NA_KH_MAX = 8
NA_KW = 16
SGU_GROUPS = 4
SGU_CHUNK = 128
N_MOD = 6
NORM_EPS = 1e-6
MASK_VALUE = -1e30
LOG2E = math.log2(math.e)

ATT_QROWS = 4
ATT_KROWS = 12
NORM_ROWS = 16
NORM_UNROLL = 4
FF1_RING = 4
MIX_RING = 4
FF2_STAGE_SLOTS = 8

V7X_VMEM_BYTES = 64 << 20


def _params(semantics, vmem_mib):
    assert (vmem_mib << 20) < V7X_VMEM_BYTES
    return pltpu.CompilerParams(dimension_semantics=semantics, vmem_limit_bytes=vmem_mib << 20)


def _tile(n, pref, align=128, divides=()):
    for t in range(min(pref, n) // align * align, 0, -align):
        if n % t == 0 and all(v % t == 0 for v in divides):
            return t
    assert not any(divides)
    return n


def _cast_specs(weights, steps, step_index_map):
    specs = []
    for wgt in weights:
        slab = wgt.shape[0] // steps
        assert wgt.shape[0] % steps == 0 and slab % 16 == 0
        specs.append(pl.BlockSpec((slab, wgt.shape[1]), step_index_map))
    return specs


def _adaln_kernel(c_ref, w_ref, b_ref, o_ref):
    c = c_ref[...]
    a = (c * jax.nn.sigmoid(c)).astype(BF16)
    o_ref[...] = jnp.dot(a, w_ref[...].astype(BF16), preferred_element_type=F32) + b_ref[...]


def _adaln(cvecs, w_ada, b_ada):
    r, d = cvecs.shape
    n = w_ada.shape[1]
    tn = _tile(n, 512)
    return pl.pallas_call(
        _adaln_kernel,
        grid=(n // tn,),
        in_specs=[
            pl.BlockSpec((r, d), lambda j: (0, 0)),
            pl.BlockSpec((d, tn), lambda j: (0, j)),
            pl.BlockSpec((1, tn), lambda j: (0, j)),
        ],
        out_specs=pl.BlockSpec((r, tn), lambda j: (0, j)),
        out_shape=jax.ShapeDtypeStruct((r, n), F32),
        compiler_params=_params(("arbitrary",), 40),
        name="adaln",
    )(cvecs, w_ada, b_ada.reshape(1, n))


def _gelu_tanh(x):
    a = -2.0 * math.sqrt(2.0 / math.pi) * LOG2E
    return x / (1.0 + jnp.exp2(x * (a + (a * 0.044715) * (x * x))))


def _norm_mod_matmul_kernel(x_hbm, nw_ref, sc_ref, sh_ref, w_ref, cs_ref, *rest,
                            rows_chunk, epilogue, gelu_from, n_cast):
    cast_in, rest = rest[:n_cast], rest[n_cast:]
    outs, cast_out, (x_buf, h_ref, x_sem) = rest[:-3 - n_cast], rest[-3 - n_cast:-3], rest[-3:]
    for src, dst in zip(cast_in, cast_out):
        dst[...] = src[...].astype(dst.dtype)

    tm = x_buf.shape[0]
    bb = pl.program_id(0)
    i = pl.program_id(1)
    j = pl.program_id(2)
    n_i = pl.num_programs(1)
    n_tiles = pl.num_programs(0) * n_i

    def x_copy(tile):
        return pltpu.make_async_copy(
            x_hbm.at[tile // n_i, pl.ds((tile % n_i) * tm, tm), :], x_buf, x_sem)

    @pl.when(j == 0)
    def _():
        tile = bb * n_i + i

        @pl.when(tile == 0)
        def _():
            x_copy(tile).start()

        x_copy(tile).wait()

        def body(c, carry):
            r = pl.multiple_of(c * rows_chunk, rows_chunk)
            x = x_buf[pl.ds(r, rows_chunk), :]
            ms = jnp.mean(x * x, axis=-1, keepdims=True)
            gain = nw_ref[...] * (1.0 + sc_ref[...])
            h = x * lax.rsqrt(ms + NORM_EPS) * gain + sh_ref[...]
            h_ref[pl.ds(r, rows_chunk), :] = h.astype(BF16)
            return carry
        lax.fori_loop(0, tm // rows_chunk, body, 0, unroll=NORM_UNROLL)

        @pl.when(tile + 1 < n_tiles)
        def _():
            x_copy(tile + 1).start()

    def matmul():
        return jnp.dot(h_ref[...], w_ref[...], preferred_element_type=F32)

    if epilogue == "plain":
        outs[0][...] = matmul().astype(BF16)
    else:
        assert epilogue == "heads_or_gelu"
        heads_ref, gated_ref = outs

        @pl.when(j < gelu_from)
        def _():
            val = (matmul() * cs_ref[...]).astype(BF16)
            for hh in range(heads_ref.shape[0]):
                heads_ref[hh] = val[:, hh * HEAD_DIM:(hh + 1) * HEAD_DIM]

        @pl.when(j >= gelu_from)
        def _():
            gated_ref[...] = _gelu_tanh(matmul()).astype(BF16)


def _norm_mod_matmul(x, norm_w, scale, shift, w, *, epilogue="plain", col_scale=None, gelu_cols=0,
                     n_lo=0, n_hi=None, cast_weights=(), tm=1024, tn=1024, vmem_mib=56):
    b, s, k = x.shape
    n_hi = w.shape[1] if n_hi is None else n_hi
    n = n_hi - n_lo
    tm = _tile(s, tm, align=8)
    tn = _tile(n, tn, divides=(n_lo, n - gelu_cols))
    j0 = n_lo // tn
    gelu_from = (n - gelu_cols) // tn
    if col_scale is None:
        col_scale = jnp.ones((n,), F32)
    per_batch = scale.shape[0] == b and b > 1
    mod_map = (lambda bb, i, j: (bb, 0, 0)) if per_batch else (lambda bb, i, j: (0, 0, 0))
    if epilogue == "heads_or_gelu":
        hpb = tn // HEAD_DIM
        out_specs = [
            pl.BlockSpec((None, hpb, tm, HEAD_DIM),
                         lambda bb, i, j: (bb, jnp.minimum(j, gelu_from - 1), i, 0)),
            pl.BlockSpec((None, tm, tn), lambda bb, i, j: (bb, i, jnp.maximum(j - gelu_from, 0))),
        ]
        out_shape = [jax.ShapeDtypeStruct((b, (n - gelu_cols) // HEAD_DIM, s, HEAD_DIM), BF16),
                     jax.ShapeDtypeStruct((b, s, gelu_cols), BF16)]
    else:
        out_specs = [pl.BlockSpec((None, tm, tn), lambda bb, i, j: (bb, i, j))]
        out_shape = [jax.ShapeDtypeStruct((b, s, n), BF16)]
    n_i, n_j = s // tm, n // tn
    cast_steps = 1 << ((b * n_i * n_j).bit_length() - 1)
    cast_specs = _cast_specs(
        cast_weights, cast_steps,
        lambda bb, i, j: (jnp.minimum((bb * n_i + i) * n_j + j, cast_steps - 1), 0))
    out = pl.pallas_call(
        functools.partial(_norm_mod_matmul_kernel, rows_chunk=min(NORM_ROWS, tm), epilogue=epilogue,
                          gelu_from=gelu_from, n_cast=len(cast_weights)),
        grid=(b, n_i, n_j),
        in_specs=[
            pl.BlockSpec(memory_space=pl.ANY),
            pl.BlockSpec((1, k), lambda bb, i, j: (0, 0)),
            pl.BlockSpec((None, 1, k), mod_map),
            pl.BlockSpec((None, 1, k), mod_map),
            pl.BlockSpec((k, tn), lambda bb, i, j: (0, j0 + j)),
            pl.BlockSpec((1, tn), lambda bb, i, j: (0, j)),
        ] + cast_specs,
        out_specs=out_specs + cast_specs,
        out_shape=out_shape + [jax.ShapeDtypeStruct(wgt.shape, BF16) for wgt in cast_weights],
        scratch_shapes=[pltpu.VMEM((tm, k), F32), pltpu.VMEM((tm, k), BF16),
                        pltpu.SemaphoreType.DMA(())],
        compiler_params=_params(("arbitrary", "arbitrary", "arbitrary"), vmem_mib),
        name="norm_mod_matmul_" + epilogue,
    )(x, norm_w.reshape(1, k), scale, shift, w, col_scale.reshape(1, n), *cast_weights)
    return out if len(out) > 1 else out[0]


def _ff1_kernel(x_hbm, nw_ref, sc_ref, sh_ref, w_ref, o_ref, x_ring, h_even, h_odd, sems, *,
                n_i, n_j, n_tiles, rows_chunk, per_batch):
    tm = h_even.shape[0]
    rs = tm // n_j
    total = n_tiles * n_j
    tile = pl.program_id(0) * n_i + pl.program_id(1)
    j = pl.program_id(2)

    def chunk_copy(g):
        t, c = g // n_j, g % n_j
        return pltpu.make_async_copy(
            x_hbm.at[t // n_i, pl.ds((t % n_i) * tm + c * rs, rs), :],
            x_ring.at[g % FF1_RING], sems.at[g % FF1_RING])

    def norm_chunk(g, dst):
        t, c = g // n_j, g % n_j
        bm = jnp.minimum(t, n_tiles - 1) // n_i if per_batch else 0
        gain = nw_ref[...] * (1.0 + sc_ref[bm])
        shift = sh_ref[bm]
        for sub in range(rs // rows_chunk):
            x = x_ring[g % FF1_RING, sub * rows_chunk:(sub + 1) * rows_chunk, :]
            ms = jnp.mean(x * x, axis=-1, keepdims=True)
            h = x * lax.rsqrt(ms + NORM_EPS) * gain + shift
            r = pl.multiple_of(c * rs + sub * rows_chunk, rows_chunk)
            dst[pl.ds(r, rows_chunk), :] = h.astype(BF16)

    @pl.when((tile == 0) & (j == 0))
    def _():
        for g in range(min(FF1_RING, total)):
            chunk_copy(g).start()
        for g in range(n_j):
            chunk_copy(g).wait()
            norm_chunk(g, h_even)
            if g + FF1_RING < total:
                chunk_copy(g + FF1_RING).start()

    g = (tile + 1) * n_j + j

    @pl.when(g < total)
    def _():
        chunk_copy(g).wait()

    def step(h_cur, h_next):
        norm_chunk(g, h_next)
        acc = jnp.dot(h_cur[...], w_ref[...], preferred_element_type=F32)
        o_ref[...] = jnp.square(jnp.maximum(acc, 0.0)).astype(o_ref.dtype)

    @pl.when(tile % 2 == 0)
    def _():
        step(h_even, h_odd)

    @pl.when(tile % 2 == 1)
    def _():
        step(h_odd, h_even)

    @pl.when(g + FF1_RING < total)
    def _():
        chunk_copy(g + FF1_RING).start()


def _ff1(x, norm_w, scale, shift, w, *, tm=1024, tn=1024):
    b, s, k = x.shape
    n = w.shape[1]
    tm = _tile(s, tm, align=8)
    tn = _tile(n, tn)
    n_i, n_j = s // tm, n // tn
    rs = tm // n_j
    rows_chunk = min(NORM_ROWS, rs)
    assert tm % n_j == 0 and rs % rows_chunk == 0 and n_j % FF1_RING == 0
    nb = scale.shape[0]
    whole = lambda bb, i, j: (0, 0, 0)
    return pl.pallas_call(
        functools.partial(_ff1_kernel, n_i=n_i, n_j=n_j, n_tiles=b * n_i, rows_chunk=rows_chunk,
                          per_batch=nb == b and b > 1),
        grid=(b, n_i, n_j),
        in_specs=[
            pl.BlockSpec(memory_space=pl.ANY),
            pl.BlockSpec((1, k), lambda bb, i, j: (0, 0)),
            pl.BlockSpec((nb, 1, k), whole),
            pl.BlockSpec((nb, 1, k), whole),
            pl.BlockSpec((k, tn), lambda bb, i, j: (0, j)),
        ],
        out_specs=pl.BlockSpec((None, tm, tn), lambda bb, i, j: (bb, i, j)),
        out_shape=jax.ShapeDtypeStruct((b, s, n), BF16),
        scratch_shapes=[pltpu.VMEM((FF1_RING, rs, k), F32), pltpu.VMEM((tm, k), BF16),
                        pltpu.VMEM((tm, k), BF16), pltpu.SemaphoreType.DMA((FF1_RING,))],
        compiler_params=_params(("arbitrary", "arbitrary", "arbitrary"), 56),
        name="ff1_norm_matmul_relu2",
    )(x, norm_w.reshape(1, k), scale, shift, w)


def _window_structure(rows, rblk):
    kh = min(NA_KH_MAX, rows)
    kstart = min(max(rblk * ATT_QROWS - kh // 2, 0), rows - ATT_KROWS)
    out = []
    for i in range(ATT_QROWS):
        qr = rblk * ATT_QROWS + i
        r0 = min(max(qr - kh // 2, 0), rows - kh)
        row = []
        for j in range(ATT_KROWS):
            kr = kstart + j
            row.append(kr - qr + NA_KH_MAX - 1 if r0 <= kr < r0 + kh else None)
        assert sum(e is not None for e in row) == kh
        out.append(tuple(row))
    return tuple(out)


def _block_types(rows):
    nblk = rows // ATT_QROWS
    reps = [0, min(1, nblk - 1), nblk - 1]
    structs = [_window_structure(rows, r) for r in reps]
    for r in range(nblk):
        t = 0 if r == 0 else (2 if r == nblk - 1 else 1)
        assert _window_structure(rows, r) == structs[t]
    return structs


def _attn_kernel(rpb_ref, q_ref, k_ref, v_ref, kc_ref, vc_ref, *rest, rows, n_cast):
    cast_in, o_ref, cast_out = rest[:n_cast], rest[n_cast], rest[n_cast + 1:2 * n_cast + 1]
    bias_ref, s_bufs, p_bufs, l_bufs = (rest[2 * n_cast + 1], rest[2 * n_cast + 2:2 * n_cast + 4],
                                        rest[2 * n_cast + 4:2 * n_cast + 6], rest[2 * n_cast + 6:])
    for src, dst in zip(cast_in, cast_out):
        dst[...] = src[...].astype(dst.dtype)

    w = GRID_W
    n_ro = 2 * NA_KH_MAX - 1
    n_co = 2 * NA_KW - 1
    nblk = rows // ATT_QROWS
    kh = min(NA_KH_MAX, rows)
    tq = ATT_QROWS * w
    tk = ATT_KROWS * w
    h = pl.program_id(0)
    b = pl.program_id(1)

    @pl.when(b == 0)
    def _():
        qc = lax.broadcasted_iota(jnp.int32, (w, 2 * w), 0)
        lane = lax.broadcasted_iota(jnp.int32, (w, 2 * w), 1)
        kc = lane % w
        cs = jnp.clip(qc - NA_KW // 2, 0, w - NA_KW)
        col_ok = (kc >= cs) & (kc < cs + NA_KW)
        diff = kc - qc + (NA_KW - 1)
        hit = [(diff == d) & col_ok for d in range(n_co)]
        base = h * (n_ro * n_co)
        neg = jnp.full((w, 2 * w), MASK_VALUE, F32)
        tables = []
        for ro in range(n_ro):
            acc = neg
            for d in range(n_co):
                acc = jnp.where(hit[d], rpb_ref[base + ro * n_co + d] * LOG2E, acc)
            tables.append(acc)
        left = lane < w
        for t, struct in enumerate(_block_types(rows)):
            for i in range(ATT_QROWS):
                for jp in range(ATT_KROWS // 2):
                    ro_l, ro_r = struct[i][2 * jp], struct[i][2 * jp + 1]
                    blk_l = neg if ro_l is None else tables[ro_l]
                    blk_r = neg if ro_r is None else tables[ro_r]
                    blk = blk_l if ro_l == ro_r else jnp.where(left, blk_l, blk_r)
                    bias_ref[t, i * w:(i + 1) * w, jp * 2 * w:(jp + 1) * 2 * w] = blk

    nt = (((1,), (1,)), ((), ()))
    last = nblk - 1

    def key_start(r):
        kstart = jnp.clip(r * ATT_QROWS - kh // 2, 0, rows - ATT_KROWS)
        return pl.multiple_of(kstart * w, w)

    def scores(r, slot):
        r = jnp.minimum(r, last)
        q = q_ref[pl.ds(pl.multiple_of(r * tq, tq), tq), :]
        kw = k_ref[pl.ds(key_start(r), tk), :]
        t = jnp.where(r == 0, 0, jnp.where(r == last, 2, 1))
        s_bufs[slot][:, 0:tk] = lax.dot_general(q, kw, nt, preferred_element_type=F32) + bias_ref[t]
        s_bufs[slot][:, tk:] = lax.dot_general(q, kc_ref[...], nt, preferred_element_type=F32)

    def softmax(slot):
        s = s_bufs[slot][...]
        p = jnp.exp2(s - jnp.max(s, axis=-1, keepdims=True))
        l_bufs[slot][...] = 1.0 / jnp.sum(p, axis=-1, keepdims=True)
        p_bufs[slot][...] = p.astype(BF16)

    def values(r, slot):
        vw = v_ref[pl.ds(key_start(r), tk), :]
        o = (jnp.dot(p_bufs[slot][:, 0:tk], vw, preferred_element_type=F32)
             + jnp.dot(p_bufs[slot][:, tk:], vc_ref[...], preferred_element_type=F32))
        o_ref[pl.ds(pl.multiple_of(r * tq, tq), tq), :] = (o * l_bufs[slot][...]).astype(o_ref.dtype)

    scores(0, 0)
    scores(1, 1)
    softmax(0)

    def body(it, carry):
        r = 2 * it
        scores(r + 2, 0)
        softmax(1)
        values(r, 0)
        scores(r + 3, 1)
        softmax(0)
        values(r + 1, 1)
        return carry
    lax.fori_loop(0, nblk // 2, body, 0, unroll=4)


def _attention(qkv, ctx_kv, rpb, cast_weights):
    b, _, s, _ = qkv.shape
    c = ctx_kv.shape[1]
    nh, dh = NA_HEADS, HEAD_DIM
    rows = s // GRID_W
    assert s % GRID_W == 0 and rows % (2 * ATT_QROWS) == 0 and rows >= ATT_KROWS
    tq, tk = ATT_QROWS * GRID_W, ATT_KROWS * GRID_W
    cast_specs = _cast_specs(cast_weights, nh * b, lambda h, bb: (h * b + bb, 0))
    return pl.pallas_call(
        functools.partial(_attn_kernel, rows=rows, n_cast=len(cast_weights)),
        grid=(nh, b),
        in_specs=[
            pl.BlockSpec(memory_space=pltpu.SMEM),
            pl.BlockSpec((None, None, s, dh), lambda h, bb: (bb, h, 0, 0)),
            pl.BlockSpec((None, None, s, dh), lambda h, bb: (bb, nh + h, 0, 0)),
            pl.BlockSpec((None, None, s, dh), lambda h, bb: (bb, 2 * nh + h, 0, 0)),
            pl.BlockSpec((None, c, dh), lambda h, bb: (bb, 0, h)),
            pl.BlockSpec((None, c, dh), lambda h, bb: (bb, 0, nh + h)),
        ] + cast_specs,
        out_specs=[pl.BlockSpec((None, s, dh), lambda h, bb: (bb, 0, h))] + cast_specs,
        out_shape=[jax.ShapeDtypeStruct((b, s, nh * dh), BF16)]
        + [jax.ShapeDtypeStruct(wgt.shape, BF16) for wgt in cast_weights],
        scratch_shapes=[pltpu.VMEM((3, tq, tk), F32)]
        + [pltpu.VMEM((tq, tk + c), F32)] * 2 + [pltpu.VMEM((tq, tk + c), BF16)] * 2
        + [pltpu.VMEM((tq, 1), F32)] * 2,
        compiler_params=_params(("arbitrary", "arbitrary"), 56),
        name="nbr_attention",
    )(rpb.reshape(-1), qkv, qkv, qkv, ctx_kv, ctx_kv, *cast_weights)


def _mix_out_kernel(gated_hbm, ona_hbm, x_ref, gate_ref, wout_ref, ws_ref, bs_ref, snw_ref,
                    gna_ref, gsg_ref, o_ref, u_ring, g_ring, ona_ring, m_even, m_odd, sems, *,
                    n_i, n_j, n_tiles):
    tm = m_even.shape[0]
    d_sgu = u_ring.shape[2]
    d_na = ona_ring.shape[2]
    gd = d_sgu // SGU_GROUPS
    total = n_tiles * n_j
    tile = pl.program_id(0) * n_i + pl.program_id(1)
    j = pl.program_id(2)

    def copies(g):
        t, c, slot = g // n_j, g % n_j, g % MIX_RING
        bb, rows = t // n_i, pl.ds((t % n_i) * tm + c * SGU_CHUNK, SGU_CHUNK)
        return (pltpu.make_async_copy(gated_hbm.at[bb, rows, pl.ds(0, d_sgu)], u_ring.at[slot],
                                      sems.at[0, slot]),
                pltpu.make_async_copy(gated_hbm.at[bb, rows, pl.ds(d_sgu, d_sgu)], g_ring.at[slot],
                                      sems.at[1, slot]),
                pltpu.make_async_copy(ona_hbm.at[bb, rows, :], ona_ring.at[slot], sems.at[2, slot]))

    def mix_chunk(g, dst):
        slot = g % MIX_RING
        rs = pl.ds(pl.multiple_of((g % n_j) * SGU_CHUNK, SGU_CHUNK), SGU_CHUNK)
        gg = g_ring[slot].astype(F32)
        ms = jnp.mean(gg * gg, axis=-1, keepdims=True)
        gn = (gg * lax.rsqrt(ms + NORM_EPS) * snw_ref[...]).astype(BF16)
        parts = []
        ssq = jnp.zeros((SGU_CHUNK, 1), F32)
        for grp in range(SGU_GROUPS):
            cols = slice(grp * gd, (grp + 1) * gd)
            mixed = jnp.dot(ws_ref[grp], gn[:, cols], preferred_element_type=F32) + bs_ref[grp]
            og = u_ring[slot, :, cols].astype(F32) * mixed
            ssq = ssq + jnp.sum(og * og, axis=-1, keepdims=True)
            parts.append(og)
        inv = lax.rsqrt(ssq / d_sgu + NORM_EPS)
        for grp in range(SGU_GROUPS):
            cols = slice(grp * gd, (grp + 1) * gd)
            dst[rs, d_na + grp * gd:d_na + (grp + 1) * gd] = (
                parts[grp] * inv * gsg_ref[:, cols]).astype(BF16)
        on = ona_ring[slot].astype(F32)
        ms = jnp.mean(on * on, axis=-1, keepdims=True)
        dst[rs, 0:d_na] = (on * lax.rsqrt(ms + NORM_EPS) * gna_ref[...]).astype(BF16)

    @pl.when((tile == 0) & (j == 0))
    def _():
        for g in range(min(MIX_RING, total)):
            for cp in copies(g):
                cp.start()
        for g in range(n_j):
            for cp in copies(g):
                cp.wait()
            mix_chunk(g, m_even)
            if g + MIX_RING < total:
                for cp in copies(g + MIX_RING):
                    cp.start()

    g = (tile + 1) * n_j + j

    @pl.when(g < total)
    def _():
        for cp in copies(g):
            cp.wait()

    def step(m_cur, m_next):
        mix_chunk(g, m_next)
        acc = jnp.dot(m_cur[...], wout_ref[...], preferred_element_type=F32)
        o_ref[...] = x_ref[...] + gate_ref[...] * acc

    @pl.when(tile % 2 == 0)
    def _():
        step(m_even, m_odd)

    @pl.when(tile % 2 == 1)
    def _():
        step(m_odd, m_even)

    @pl.when(g + MIX_RING < total)
    def _():
        for cp in copies(g + MIX_RING):
            cp.start()


def _mix_out(gated, o_na, x, gate, w_out, sgu_w, sgu_b, sgu_norm_w, gn_na, gn_sgu, *, tm=1024):
    b, s, d = x.shape
    d_na = o_na.shape[2]
    d_sgu = d - d_na
    tm = _tile(s, tm)
    n_i, n_j = s // tm, tm // SGU_CHUNK
    tn = d // n_j
    assert (tm % SGU_CHUNK == 0 and d % n_j == 0 and tn % 128 == 0 and n_j % MIX_RING == 0
            and gated.shape[2] == 2 * d_sgu)
    const2 = lambda bb, i, j: (0, 0)
    const3 = lambda bb, i, j: (0, 0, 0)
    return pl.pallas_call(
        functools.partial(_mix_out_kernel, n_i=n_i, n_j=n_j, n_tiles=b * n_i),
        grid=(b, n_i, n_j),
        in_specs=[
            pl.BlockSpec(memory_space=pl.ANY),
            pl.BlockSpec(memory_space=pl.ANY),
            pl.BlockSpec((None, tm, tn), lambda bb, i, j: (bb, i, j)),
            pl.BlockSpec((None, 1, tn), lambda bb, i, j: (bb, 0, j)),
            pl.BlockSpec((d, tn), lambda bb, i, j: (0, j)),
            pl.BlockSpec((SGU_GROUPS, SGU_CHUNK, SGU_CHUNK), const3),
            pl.BlockSpec((SGU_GROUPS, SGU_CHUNK, 1), const3),
            pl.BlockSpec((1, d_sgu), const2),
            pl.BlockSpec((1, d_na), const2),
            pl.BlockSpec((1, d_sgu), const2),
        ],
        out_specs=pl.BlockSpec((None, tm, tn), lambda bb, i, j: (bb, i, j)),
        out_shape=jax.ShapeDtypeStruct((b, s, d), F32),
        scratch_shapes=[pltpu.VMEM((MIX_RING, SGU_CHUNK, d_sgu), BF16),
                        pltpu.VMEM((MIX_RING, SGU_CHUNK, d_sgu), BF16),
                        pltpu.VMEM((MIX_RING, SGU_CHUNK, d_na), BF16),
                        pltpu.VMEM((tm, d), BF16), pltpu.VMEM((tm, d), BF16),
                        pltpu.SemaphoreType.DMA((3, MIX_RING))],
        compiler_params=_params(("arbitrary", "arbitrary", "arbitrary"), 48),
        name="mix_out",
    )(gated, o_na, x, gate, w_out, sgu_w.astype(BF16),
      sgu_b.reshape(SGU_GROUPS, SGU_CHUNK, 1), sgu_norm_w.reshape(1, d_sgu),
      gn_na.reshape(1, d_na), gn_sgu.reshape(1, d_sgu))


def _ff2_norm_kernel(a_ref, w_ref, x_ref, g_ref, fw_ref, o_hbm, acc_ref, row_buf, stage, sems, *,
                     rows_chunk):
    n_j, tm, tn = row_buf.shape
    n_slots, _, group, _ = stage.shape
    n_groups = tm // group
    bb = pl.program_id(0)
    i = pl.program_id(1)
    j = pl.program_id(2)
    kk = pl.program_id(3)
    last_k = pl.num_programs(3) - 1
    tile = bb * pl.num_programs(1) + i
    last_tile = pl.num_programs(0) * pl.num_programs(1) - 1

    def stage_copy(grp, jj):
        return pltpu.make_async_copy(
            stage.at[grp % n_slots, jj],
            o_hbm.at[bb, pl.ds(i * tm + grp * group, group), pl.ds(jj * tn, tn)],
            sems.at[grp % n_slots, jj])

    def partial_product():
        return jnp.dot(a_ref[...], w_ref[...], preferred_element_type=F32)

    @pl.when(kk == 0)
    def _():
        acc_ref[...] = partial_product()

    @pl.when((kk > 0) & (kk < last_k))
    def _():
        acc_ref[...] += partial_product()

    @pl.when(kk == last_k)
    def _():
        row_buf[j] = x_ref[...] + g_ref[...] * (acc_ref[...] + partial_product())

    @pl.when((kk == last_k) & (j == n_j - 1))
    def _():
        def body(grp, carry):
            @pl.when((grp >= n_slots) | (tile > 0))
            def _():
                for jj in range(n_j):
                    stage_copy(grp, jj).wait()

            base = pl.multiple_of(grp * group, group)
            for sub in range(group // rows_chunk):
                rows = pl.ds(base + sub * rows_chunk, rows_chunk)
                parts = [row_buf[jj, rows, :] for jj in range(n_j)]
                ssq = sum(jnp.sum(p * p, axis=-1, keepdims=True) for p in parts)
                inv = lax.rsqrt(ssq / (n_j * tn) + NORM_EPS)
                for jj in range(n_j):
                    stage[grp % n_slots, jj, sub * rows_chunk:(sub + 1) * rows_chunk, :] = (
                        parts[jj] * inv * fw_ref[:, jj * tn:(jj + 1) * tn])
            for jj in range(n_j):
                stage_copy(grp, jj).start()
            return carry
        lax.fori_loop(0, n_groups, body, 0)

        @pl.when(tile == last_tile)
        def _():
            for grp in range(n_groups - n_slots, n_groups):
                for jj in range(n_j):
                    stage_copy(grp, jj).wait()


def _ff2_norm(a, w, x, gate, final_w, *, tm=1024, tn=1024, tk=2048):
    b, s, k = a.shape
    n = w.shape[1]
    tm, tn, tk = _tile(s, tm), _tile(n, tn), _tile(k, tk)
    n_j = n // tn
    group = NORM_UNROLL * NORM_ROWS
    n_groups = tm // group
    assert k // tk >= 2
    assert tm % group == 0 and n_groups % FF2_STAGE_SLOTS == 0
    return pl.pallas_call(
        functools.partial(_ff2_norm_kernel, rows_chunk=NORM_ROWS),
        grid=(b, s // tm, n // tn, k // tk),
        in_specs=[
            pl.BlockSpec((None, tm, tk), lambda bb, i, j, kk: (bb, i, kk)),
            pl.BlockSpec((tk, tn), lambda bb, i, j, kk: (kk, j)),
            pl.BlockSpec((None, tm, tn), lambda bb, i, j, kk: (bb, i, j)),
            pl.BlockSpec((None, 1, tn), lambda bb, i, j, kk: (bb, 0, j)),
            pl.BlockSpec((1, n), lambda bb, i, j, kk: (0, 0)),
        ],
        out_specs=pl.BlockSpec(memory_space=pl.ANY),
        out_shape=jax.ShapeDtypeStruct((b, s, n), F32),
        scratch_shapes=[pltpu.VMEM((tm, tn), F32), pltpu.VMEM((n_j, tm, tn), F32),
                        pltpu.VMEM((FF2_STAGE_SLOTS, n_j, group, tn), F32),
                        pltpu.SemaphoreType.DMA((FF2_STAGE_SLOTS, n_j))],
        compiler_params=_params(("arbitrary", "arbitrary", "arbitrary", "arbitrary"), 60),
        name="ff2_residual_norm",
    )(a, w, x, gate, final_w.reshape(1, n))


def kernel(x, c, ctx, c_ctx, w_ada, b_ada, norm1_w, w_in, rpb, sgu_norm_w, sgu_w, sgu_b,
           grp_norm_na, grp_norm_sgu, w_out, norm2_w, w_ff1, w_ff2, final_norm_w):
    b, s, d = x.shape
    n_ctx = ctx.shape[1]
    d_na = NA_HEADS * HEAD_DIM
    d_sgu = d - d_na
    depth = w_ada.shape[0]
    assert depth == 1

    n_rows = -(-(b + 1) // 8) * 8
    cvecs = jnp.zeros((n_rows, d), F32).at[:b].set(c).at[b].set(c_ctx)
    mod = _adaln(cvecs, w_ada[0], b_ada[0]).reshape(n_rows, N_MOD, d)
    lat = lambda i: mod[:b, i].reshape(b, 1, d)
    cx = lambda i: mod[b, i].reshape(1, 1, d)

    w_in_b = w_in[0].astype(BF16)
    qkv_scale = jnp.concatenate([jnp.full((d_na,), HEAD_DIM ** -0.5 * LOG2E, F32),
                                 jnp.ones((2 * d_na + 2 * d_sgu,), F32)])
    qkv, gated, w_out_b, w_ff1_b = _norm_mod_matmul(
        x, norm1_w[0], lat(1), lat(0), w_in_b, epilogue="heads_or_gelu", col_scale=qkv_scale,
        gelu_cols=2 * d_sgu, cast_weights=[w_out[0], w_ff1[0]], tn=512, vmem_mib=60)
    ctx_kv = _norm_mod_matmul(ctx.reshape(1, b * n_ctx, d), norm1_w[0], cx(1), cx(0), w_in_b,
                              n_lo=d_na, n_hi=3 * d_na).reshape(b, n_ctx, 2 * d_na)
    o_na, w_ff2_b = _attention(qkv, ctx_kv, rpb[0], [w_ff2[0]])
    x1 = _mix_out(gated, o_na, x, lat(2), w_out_b, sgu_w[0], sgu_b[0], sgu_norm_w[0],
                  grp_norm_na[0], grp_norm_sgu[0])
    hidden = _ff1(x1, norm2_w[0], lat(4), lat(3), w_ff1_b)
    return _ff2_norm(hidden, w_ff2_b, x1, lat(5), final_norm_w)
```

```python
import functools
import math

import jax
import jax.numpy as jnp
from jax import lax
from jax.experimental import pallas as pl
from jax.experimental.pallas import tpu as pltpu

F32 = jnp.float32
BF16 = jnp.bfloat16

GRID_W = 64
NA_HEADS = 16
HEAD_DIM = 128
NA_KH_MAX = 8
NA_KW = 16
SGU_GROUPS = 4
SGU_CHUNK = 128
N_MOD = 6
NORM_EPS = 1e-6
MASK_VALUE = -1e30
LOG2E = math.log2(math.e)

ATT_QROWS = 4
ATT_KROWS = 12
NORM_ROWS = 16
NORM_UNROLL = 4
FF1_RING = 4
MIX_RING = 4
FF2_STAGE_SLOTS = 8

V7X_VMEM_BYTES = 64 << 20


def _params(semantics, vmem_mib):
    assert (vmem_mib << 20) < V7X_VMEM_BYTES
    return pltpu.CompilerParams(dimension_semantics=semantics, vmem_limit_bytes=vmem_mib << 20)


def _tile(n, pref, align=128, divides=()):
    for t in range(min(pref, n) // align * align, 0, -align):
        if n % t == 0 and all(v % t == 0 for v in divides):
            return t
    assert not any(divides)
    return n


def _cast_specs(weights, steps, step_index_map):
    specs = []
    for wgt in weights:
        slab = wgt.shape[0] // steps
        assert wgt.shape[0] % steps == 0 and slab % 16 == 0
        specs.append(pl.BlockSpec((slab, wgt.shape[1]), step_index_map))
    return specs


def _adaln_kernel(c_ref, w_ref, b_ref, o_ref):
    c = c_ref[...]
    a = (c * jax.nn.sigmoid(c)).astype(BF16)
    o_ref[...] = jnp.dot(a, w_ref[...].astype(BF16), preferred_element_type=F32) + b_ref[...]


def _adaln(cvecs, w_ada, b_ada):
    r, d = cvecs.shape
    n = w_ada.shape[1]
    tn = _tile(n, 512)
    return pl.pallas_call(
        _adaln_kernel,
        grid=(n // tn,),
        in_specs=[
            pl.BlockSpec((r, d), lambda j: (0, 0)),
            pl.BlockSpec((d, tn), lambda j: (0, j)),
            pl.BlockSpec((1, tn), lambda j: (0, j)),
        ],
        out_specs=pl.BlockSpec((r, tn), lambda j: (0, j)),
        out_shape=jax.ShapeDtypeStruct((r, n), F32),
        compiler_params=_params(("arbitrary",), 40),
        name="adaln",
    )(cvecs, w_ada, b_ada.reshape(1, n))


def _gelu_tanh(x):
    a = -2.0 * math.sqrt(2.0 / math.pi) * LOG2E
    return x / (1.0 + jnp.exp2(x * (a + (a * 0.044715) * (x * x))))


def _norm_mod_matmul_kernel(x_hbm, nw_ref, sc_ref, sh_ref, w_ref, cs_ref, *rest,
                            rows_chunk, epilogue, gelu_from):
    outs, (x_buf, h_ref, x_sem) = rest[:-3], rest[-3:]
    tm = x_buf.shape[0]
    bb = pl.program_id(0)
    i = pl.program_id(1)
    j = pl.program_id(2)
    n_i = pl.num_programs(1)
    n_tiles = pl.num_programs(0) * n_i

    def x_copy(tile):
        return pltpu.make_async_copy(
            x_hbm.at[tile // n_i, pl.ds((tile % n_i) * tm, tm), :], x_buf, x_sem)

    @pl.when(j == 0)
    def _():
        tile = bb * n_i + i

        @pl.when(tile == 0)
        def _():
            x_copy(tile).start()

        x_copy(tile).wait()

        def body(c, carry):
            r = pl.multiple_of(c * rows_chunk, rows_chunk)
            x = x_buf[pl.ds(r, rows_chunk), :]
            ms = jnp.mean(x * x, axis=-1, keepdims=True)
            gain = nw_ref[...] * (1.0 + sc_ref[...])
            h = x * lax.rsqrt(ms + NORM_EPS) * gain + sh_ref[...]
            h_ref[pl.ds(r, rows_chunk), :] = h.astype(BF16)
            return carry
        lax.fori_loop(0, tm // rows_chunk, body, 0, unroll=NORM_UNROLL)

        @pl.when(tile + 1 < n_tiles)
        def _():
            x_copy(tile + 1).start()

    def matmul():
        return jnp.dot(h_ref[...], w_ref[...], preferred_element_type=F32)

    if epilogue == "plain":
        outs[0][...] = matmul().astype(BF16)
    else:
        assert epilogue == "heads_or_gelu"
        heads_ref, gated_ref = outs

        @pl.when(j < gelu_from)
        def _():
            val = (matmul() * cs_ref[...]).astype(BF16)
            for hh in range(heads_ref.shape[0]):
                heads_ref[hh] = val[:, hh * HEAD_DIM:(hh + 1) * HEAD_DIM]

        @pl.when(j >= gelu_from)
        def _():
            gated_ref[...] = _gelu_tanh(matmul()).astype(BF16)


def _norm_mod_matmul(x, norm_w, scale, shift, w, *, epilogue="plain", col_scale=None, gelu_cols=0,
                     n_lo=0, n_hi=None, tm=1024, tn=1024, vmem_mib=56):
    b, s, k = x.shape
    n_hi = w.shape[1] if n_hi is None else n_hi
    n = n_hi - n_lo
    tm = _tile(s, tm, align=8)
    tn = _tile(n, tn, divides=(n_lo, n - gelu_cols))
    j0 = n_lo // tn
    gelu_from = (n - gelu_cols) // tn
    if col_scale is None:
        col_scale = jnp.ones((n,), F32)
    per_batch = scale.shape[0] == b and b > 1
    mod_map = (lambda bb, i, j: (bb, 0, 0)) if per_batch else (lambda bb, i, j: (0, 0, 0))
    if epilogue == "heads_or_gelu":
        hpb = tn // HEAD_DIM
        out_specs = [
            pl.BlockSpec((None, hpb, tm, HEAD_DIM),
                         lambda bb, i, j: (bb, jnp.minimum(j, gelu_from - 1), i, 0)),
            pl.BlockSpec((None, tm, tn), lambda bb, i, j: (bb, i, jnp.maximum(j - gelu_from, 0))),
        ]
        out_shape = [jax.ShapeDtypeStruct((b, (n - gelu_cols) // HEAD_DIM, s, HEAD_DIM), BF16),
                     jax.ShapeDtypeStruct((b, s, gelu_cols), BF16)]
    else:
        out_specs = [pl.BlockSpec((None, tm, tn), lambda bb, i, j: (bb, i, j))]
        out_shape = [jax.ShapeDtypeStruct((b, s, n), BF16)]
    out = pl.pallas_call(
        functools.partial(_norm_mod_matmul_kernel, rows_chunk=min(NORM_ROWS, tm), epilogue=epilogue,
                          gelu_from=gelu_from),
        grid=(b, s // tm, n // tn),
        in_specs=[
            pl.BlockSpec(memory_space=pl.ANY),
            pl.BlockSpec((1, k), lambda bb, i, j: (0, 0)),
            pl.BlockSpec((None, 1, k), mod_map),
            pl.BlockSpec((None, 1, k), mod_map),
            pl.BlockSpec((k, tn), lambda bb, i, j: (0, j0 + j)),
            pl.BlockSpec((1, tn), lambda bb, i, j: (0, j)),
        ],
        out_specs=out_specs,
        out_shape=out_shape,
        scratch_shapes=[pltpu.VMEM((tm, k), F32), pltpu.VMEM((tm, k), BF16),
                        pltpu.SemaphoreType.DMA(())],
        compiler_params=_params(("arbitrary", "arbitrary", "arbitrary"), vmem_mib),
        name="norm_mod_matmul_" + epilogue,
    )(x, norm_w.reshape(1, k), scale, shift, w, col_scale.reshape(1, n))
    return out if len(out) > 1 else out[0]


def _ff1_kernel(x_hbm, nw_ref, sc_ref, sh_ref, w_ref, o_ref, x_ring, h_even, h_odd, sems, *,
                n_i, n_j, n_tiles, rows_chunk, per_batch):
    tm = h_even.shape[0]
    rs = tm // n_j
    total = n_tiles * n_j
    tile = pl.program_id(0) * n_i + pl.program_id(1)
    j = pl.program_id(2)

    def chunk_copy(g):
        t, c = g // n_j, g % n_j
        return pltpu.make_async_copy(
            x_hbm.at[t // n_i, pl.ds((t % n_i) * tm + c * rs, rs), :],
            x_ring.at[g % FF1_RING], sems.at[g % FF1_RING])

    def norm_chunk(g, dst):
        t, c = g // n_j, g % n_j
        bm = jnp.minimum(t, n_tiles - 1) // n_i if per_batch else 0
        gain = nw_ref[...] * (1.0 + sc_ref[bm])
        shift = sh_ref[bm]
        for sub in range(rs // rows_chunk):
            x = x_ring[g % FF1_RING, sub * rows_chunk:(sub + 1) * rows_chunk, :]
            ms = jnp.mean(x * x, axis=-1, keepdims=True)
            h = x * lax.rsqrt(ms + NORM_EPS) * gain + shift
            r = pl.multiple_of(c * rs + sub * rows_chunk, rows_chunk)
            dst[pl.ds(r, rows_chunk), :] = h.astype(BF16)

    @pl.when((tile == 0) & (j == 0))
    def _():
        for g in range(min(FF1_RING, total)):
            chunk_copy(g).start()
        for g in range(n_j):
            chunk_copy(g).wait()
            norm_chunk(g, h_even)
            if g + FF1_RING < total:
                chunk_copy(g + FF1_RING).start()

    g = (tile + 1) * n_j + j

    @pl.when(g < total)
    def _():
        chunk_copy(g).wait()

    def step(h_cur, h_next):
        norm_chunk(g, h_next)
        acc = jnp.dot(h_cur[...], w_ref[...], preferred_element_type=F32)
        o_ref[...] = jnp.square(jnp.maximum(acc, 0.0)).astype(o_ref.dtype)

    @pl.when(tile % 2 == 0)
    def _():
        step(h_even, h_odd)

    @pl.when(tile % 2 == 1)
    def _():
        step(h_odd, h_even)

    @pl.when(g + FF1_RING < total)
    def _():
        chunk_copy(g + FF1_RING).start()


def _ff1(x, norm_w, scale, shift, w, *, tm=1024, tn=1024):
    b, s, k = x.shape
    n = w.shape[1]
    tm = _tile(s, tm, align=8)
    tn = _tile(n, tn)
    n_i, n_j = s // tm, n // tn
    rs = tm // n_j
    rows_chunk = min(NORM_ROWS, rs)
    assert tm % n_j == 0 and rs % rows_chunk == 0 and n_j % FF1_RING == 0
    nb = scale.shape[0]
    whole = lambda bb, i, j: (0, 0, 0)
    return pl.pallas_call(
        functools.partial(_ff1_kernel, n_i=n_i, n_j=n_j, n_tiles=b * n_i, rows_chunk=rows_chunk,
                          per_batch=nb == b and b > 1),
        grid=(b, n_i, n_j),
        in_specs=[
            pl.BlockSpec(memory_space=pl.ANY),
            pl.BlockSpec((1, k), lambda bb, i, j: (0, 0)),
            pl.BlockSpec((nb, 1, k), whole),
            pl.BlockSpec((nb, 1, k), whole),
            pl.BlockSpec((k, tn), lambda bb, i, j: (0, j)),
        ],
        out_specs=pl.BlockSpec((None, tm, tn), lambda bb, i, j: (bb, i, j)),
        out_shape=jax.ShapeDtypeStruct((b, s, n), BF16),
        scratch_shapes=[pltpu.VMEM((FF1_RING, rs, k), F32), pltpu.VMEM((tm, k), BF16),
                        pltpu.VMEM((tm, k), BF16), pltpu.SemaphoreType.DMA((FF1_RING,))],
        compiler_params=_params(("arbitrary", "arbitrary", "arbitrary"), 56),
        name="ff1_norm_matmul_relu2",
    )(x, norm_w.reshape(1, k), scale, shift, w)


def _window_structure(rows, rblk):
    kh = min(NA_KH_MAX, rows)
    kstart = min(max(rblk * ATT_QROWS - kh // 2, 0), rows - ATT_KROWS)
    out = []
    for i in range(ATT_QROWS):
        qr = rblk * ATT_QROWS + i
        r0 = min(max(qr - kh // 2, 0), rows - kh)
        row = []
        for j in range(ATT_KROWS):
            kr = kstart + j
            row.append(kr - qr + NA_KH_MAX - 1 if r0 <= kr < r0 + kh else None)
        assert sum(e is not None for e in row) == kh
        out.append(tuple(row))
    return tuple(out)


def _block_types(rows):
    nblk = rows // ATT_QROWS
    reps = [0, min(1, nblk - 1), nblk - 1]
    structs = [_window_structure(rows, r) for r in reps]
    for r in range(nblk):
        t = 0 if r == 0 else (2 if r == nblk - 1 else 1)
        assert _window_structure(rows, r) == structs[t]
    return structs


def _attn_kernel(rpb_ref, q_ref, k_ref, v_ref, kc_ref, vc_ref, *rest, rows, n_cast):
    cast_in, o_ref, cast_out = rest[:n_cast], rest[n_cast], rest[n_cast + 1:2 * n_cast + 1]
    bias_ref, s_bufs, p_bufs, l_bufs = (rest[2 * n_cast + 1], rest[2 * n_cast + 2:2 * n_cast + 4],
                                        rest[2 * n_cast + 4:2 * n_cast + 6], rest[2 * n_cast + 6:])
    for src, dst in zip(cast_in, cast_out):
        dst[...] = src[...].astype(dst.dtype)

    w = GRID_W
    n_ro = 2 * NA_KH_MAX - 1
    n_co = 2 * NA_KW - 1
    nblk = rows // ATT_QROWS
    kh = min(NA_KH_MAX, rows)
    tq = ATT_QROWS * w
    tk = ATT_KROWS * w
    h = pl.program_id(0)
    b = pl.program_id(1)

    @pl.when(b == 0)
    def _():
        qc = lax.broadcasted_iota(jnp.int32, (w, 2 * w), 0)
        lane = lax.broadcasted_iota(jnp.int32, (w, 2 * w), 1)
        kc = lane % w
        cs = jnp.clip(qc - NA_KW // 2, 0, w - NA_KW)
        col_ok = (kc >= cs) & (kc < cs + NA_KW)
        diff = kc - qc + (NA_KW - 1)
        hit = [(diff == d) & col_ok for d in range(n_co)]
        base = h * (n_ro * n_co)
        neg = jnp.full((w, 2 * w), MASK_VALUE, F32)
        tables = []
        for ro in range(n_ro):
            acc = neg
            for d in range(n_co):
                acc = jnp.where(hit[d], rpb_ref[base + ro * n_co + d] * LOG2E, acc)
            tables.append(acc)
        left = lane < w
        for t, struct in enumerate(_block_types(rows)):
            for i in range(ATT_QROWS):
                for jp in range(ATT_KROWS // 2):
                    ro_l, ro_r = struct[i][2 * jp], struct[i][2 * jp + 1]
                    blk_l = neg if ro_l is None else tables[ro_l]
                    blk_r = neg if ro_r is None else tables[ro_r]
                    blk = blk_l if ro_l == ro_r else jnp.where(left, blk_l, blk_r)
                    bias_ref[t, i * w:(i + 1) * w, jp * 2 * w:(jp + 1) * 2 * w] = blk

    nt = (((1,), (1,)), ((), ()))
    last = nblk - 1

    def key_start(r):
        kstart = jnp.clip(r * ATT_QROWS - kh // 2, 0, rows - ATT_KROWS)
        return pl.multiple_of(kstart * w, w)

    def scores(r, slot):
        r = jnp.minimum(r, last)
        q = q_ref[pl.ds(pl.multiple_of(r * tq, tq), tq), :]
        kw = k_ref[pl.ds(key_start(r), tk), :]
        t = jnp.where(r == 0, 0, jnp.where(r == last, 2, 1))
        s_bufs[slot][:, 0:tk] = lax.dot_general(q, kw, nt, preferred_element_type=F32) + bias_ref[t]
        s_bufs[slot][:, tk:] = lax.dot_general(q, kc_ref[...], nt, preferred_element_type=F32)

    def softmax(slot):
        s = s_bufs[slot][...]
        p = jnp.exp2(s - jnp.max(s, axis=-1, keepdims=True))
        l_bufs[slot][...] = 1.0 / jnp.sum(p, axis=-1, keepdims=True)
        p_bufs[slot][...] = p.astype(BF16)

    def values(r, slot):
        vw = v_ref[pl.ds(key_start(r), tk), :]
        o = (jnp.dot(p_bufs[slot][:, 0:tk], vw, preferred_element_type=F32)
             + jnp.dot(p_bufs[slot][:, tk:], vc_ref[...], preferred_element_type=F32))
        o_ref[pl.ds(pl.multiple_of(r * tq, tq), tq), :] = (o * l_bufs[slot][...]).astype(o_ref.dtype)

    scores(0, 0)
    scores(1, 1)
    softmax(0)

    def body(it, carry):
        r = 2 * it
        scores(r + 2, 0)
        softmax(1)
        values(r, 0)
        scores(r + 3, 1)
        softmax(0)
        values(r + 1, 1)
        return carry
    lax.fori_loop(0, nblk // 2, body, 0, unroll=4)


def _attention(qkv, ctx_kv, rpb, cast_weights):
    b, _, s, _ = qkv.shape
    c = ctx_kv.shape[1]
    nh, dh = NA_HEADS, HEAD_DIM
    rows = s // GRID_W
    assert s % GRID_W == 0 and rows % (2 * ATT_QROWS) == 0 and rows >= ATT_KROWS
    tq, tk = ATT_QROWS * GRID_W, ATT_KROWS * GRID_W
    cast_specs = _cast_specs(cast_weights, nh * b, lambda h, bb: (h * b + bb, 0))
    return pl.pallas_call(
        functools.partial(_attn_kernel, rows=rows, n_cast=len(cast_weights)),
        grid=(nh, b),
        in_specs=[
            pl.BlockSpec(memory_space=pltpu.SMEM),
            pl.BlockSpec((None, None, s, dh), lambda h, bb: (bb, h, 0, 0)),
            pl.BlockSpec((None, None, s, dh), lambda h, bb: (bb, nh + h, 0, 0)),
            pl.BlockSpec((None, None, s, dh), lambda h, bb: (bb, 2 * nh + h, 0, 0)),
            pl.BlockSpec((None, c, dh), lambda h, bb: (bb, 0, h)),
            pl.BlockSpec((None, c, dh), lambda h, bb: (bb, 0, nh + h)),
        ] + cast_specs,
        out_specs=[pl.BlockSpec((None, s, dh), lambda h, bb: (bb, 0, h))] + cast_specs,
        out_shape=[jax.ShapeDtypeStruct((b, s, nh * dh), BF16)]
        + [jax.ShapeDtypeStruct(wgt.shape, BF16) for wgt in cast_weights],
        scratch_shapes=[pltpu.VMEM((3, tq, tk), F32)]
        + [pltpu.VMEM((tq, tk + c), F32)] * 2 + [pltpu.VMEM((tq, tk + c), BF16)] * 2
        + [pltpu.VMEM((tq, 1), F32)] * 2,
        compiler_params=_params(("arbitrary", "arbitrary"), 56),
        name="nbr_attention",
    )(rpb.reshape(-1), qkv, qkv, qkv, ctx_kv, ctx_kv, *cast_weights)


def _mix_out_kernel(gated_hbm, ona_hbm, x_ref, gate_ref, wout_ref, ws_ref, bs_ref, snw_ref,
                    gna_ref, gsg_ref, o_ref, u_ring, g_ring, ona_ring, m_even, m_odd, sems, *,
                    n_i, n_j, n_tiles):
    tm = m_even.shape[0]
    d_sgu = u_ring.shape[2]
    d_na = ona_ring.shape[2]
    gd = d_sgu // SGU_GROUPS
    total = n_tiles * n_j
    tile = pl.program_id(0) * n_i + pl.program_id(1)
    j = pl.program_id(2)

    def copies(g):
        t, c, slot = g // n_j, g % n_j, g % MIX_RING
        bb, rows = t // n_i, pl.ds((t % n_i) * tm + c * SGU_CHUNK, SGU_CHUNK)
        return (pltpu.make_async_copy(gated_hbm.at[bb, rows, pl.ds(0, d_sgu)], u_ring.at[slot],
                                      sems.at[0, slot]),
                pltpu.make_async_copy(gated_hbm.at[bb, rows, pl.ds(d_sgu, d_sgu)], g_ring.at[slot],
                                      sems.at[1, slot]),
                pltpu.make_async_copy(ona_hbm.at[bb, rows, :], ona_ring.at[slot], sems.at[2, slot]))

    def mix_chunk(g, dst):
        slot = g % MIX_RING
        rs = pl.ds(pl.multiple_of((g % n_j) * SGU_CHUNK, SGU_CHUNK), SGU_CHUNK)
        gg = g_ring[slot].astype(F32)
        ms = jnp.mean(gg * gg, axis=-1, keepdims=True)
        gn = (gg * lax.rsqrt(ms + NORM_EPS) * snw_ref[...]).astype(BF16)
        parts = []
        ssq = jnp.zeros((SGU_CHUNK, 1), F32)
        for grp in range(SGU_GROUPS):
            cols = slice(grp * gd, (grp + 1) * gd)
            mixed = jnp.dot(ws_ref[grp], gn[:, cols], preferred_element_type=F32) + bs_ref[grp]
            og = u_ring[slot, :, cols].astype(F32) * mixed
            ssq = ssq + jnp.sum(og * og, axis=-1, keepdims=True)
            parts.append(og)
        inv = lax.rsqrt(ssq / d_sgu + NORM_EPS)
        for grp in range(SGU_GROUPS):
            cols = slice(grp * gd, (grp + 1) * gd)
            dst[rs, d_na + grp * gd:d_na + (grp + 1) * gd] = (
                parts[grp] * inv * gsg_ref[:, cols]).astype(BF16)
        on = ona_ring[slot].astype(F32)
        ms = jnp.mean(on * on, axis=-1, keepdims=True)
        dst[rs, 0:d_na] = (on * lax.rsqrt(ms + NORM_EPS) * gna_ref[...]).astype(BF16)

    @pl.when((tile == 0) & (j == 0))
    def _():
        for g in range(min(MIX_RING, total)):
            for cp in copies(g):
                cp.start()
        for g in range(n_j):
            for cp in copies(g):
                cp.wait()
            mix_chunk(g, m_even)
            if g + MIX_RING < total:
                for cp in copies(g + MIX_RING):
                    cp.start()

    g = (tile + 1) * n_j + j

    @pl.when(g < total)
    def _():
        for cp in copies(g):
            cp.wait()

    def step(m_cur, m_next):
        mix_chunk(g, m_next)
        acc = jnp.dot(m_cur[...], wout_ref[...], preferred_element_type=F32)
        o_ref[...] = x_ref[...] + gate_ref[...] * acc

    @pl.when(tile % 2 == 0)
    def _():
        step(m_even, m_odd)

    @pl.when(tile % 2 == 1)
    def _():
        step(m_odd, m_even)

    @pl.when(g + MIX_RING < total)
    def _():
        for cp in copies(g + MIX_RING):
            cp.start()


def _mix_out(gated, o_na, x, gate, w_out, sgu_w, sgu_b, sgu_norm_w, gn_na, gn_sgu, *, tm=1024):
    b, s, d = x.shape
    d_na = o_na.shape[2]
    d_sgu = d - d_na
    tm = _tile(s, tm)
    n_i, n_j = s // tm, tm // SGU_CHUNK
    tn = d // n_j
    assert (tm % SGU_CHUNK == 0 and d % n_j == 0 and tn % 128 == 0 and n_j % MIX_RING == 0
            and gated.shape[2] == 2 * d_sgu)
    const2 = lambda bb, i, j: (0, 0)
    const3 = lambda bb, i, j: (0, 0, 0)
    return pl.pallas_call(
        functools.partial(_mix_out_kernel, n_i=n_i, n_j=n_j, n_tiles=b * n_i),
        grid=(b, n_i, n_j),
        in_specs=[
            pl.BlockSpec(memory_space=pl.ANY),
            pl.BlockSpec(memory_space=pl.ANY),
            pl.BlockSpec((None, tm, tn), lambda bb, i, j: (bb, i, j)),
            pl.BlockSpec((None, 1, tn), lambda bb, i, j: (bb, 0, j)),
            pl.BlockSpec((d, tn), lambda bb, i, j: (0, j)),
            pl.BlockSpec((SGU_GROUPS, SGU_CHUNK, SGU_CHUNK), const3),
            pl.BlockSpec((SGU_GROUPS, SGU_CHUNK, 1), const3),
            pl.BlockSpec((1, d_sgu), const2),
            pl.BlockSpec((1, d_na), const2),
            pl.BlockSpec((1, d_sgu), const2),
        ],
        out_specs=pl.BlockSpec((None, tm, tn), lambda bb, i, j: (bb, i, j)),
        out_shape=jax.ShapeDtypeStruct((b, s, d), F32),
        scratch_shapes=[pltpu.VMEM((MIX_RING, SGU_CHUNK, d_sgu), BF16),
                        pltpu.VMEM((MIX_RING, SGU_CHUNK, d_sgu), BF16),
                        pltpu.VMEM((MIX_RING, SGU_CHUNK, d_na), BF16),
                        pltpu.VMEM((tm, d), BF16), pltpu.VMEM((tm, d), BF16),
                        pltpu.SemaphoreType.DMA((3, MIX_RING))],
        compiler_params=_params(("arbitrary", "arbitrary", "arbitrary"), 48),
        name="mix_out",
    )(gated, o_na, x, gate, w_out, sgu_w.astype(BF16),
      sgu_b.reshape(SGU_GROUPS, SGU_CHUNK, 1), sgu_norm_w.reshape(1, d_sgu),
      gn_na.reshape(1, d_na), gn_sgu.reshape(1, d_sgu))


def _ff2_norm_kernel(a_ref, w_ref, x_ref, g_ref, fw_ref, o_hbm, acc_ref, row_buf, stage, sems, *,
                     rows_chunk):
    n_j, tm, tn = row_buf.shape
    n_slots, _, group, _ = stage.shape
    n_groups = tm // group
    bb = pl.program_id(0)
    i = pl.program_id(1)
    j = pl.program_id(2)
    kk = pl.program_id(3)
    last_k = pl.num_programs(3) - 1
    tile = bb * pl.num_programs(1) + i
    last_tile = pl.num_programs(0) * pl.num_programs(1) - 1

    def stage_copy(grp, jj):
        return pltpu.make_async_copy(
            stage.at[grp % n_slots, jj],
            o_hbm.at[bb, pl.ds(i * tm + grp * group, group), pl.ds(jj * tn, tn)],
            sems.at[grp % n_slots, jj])

    def partial_product():
        return jnp.dot(a_ref[...], w_ref[...], preferred_element_type=F32)

    @pl.when(kk == 0)
    def _():
        acc_ref[...] = partial_product()

    @pl.when((kk > 0) & (kk < last_k))
    def _():
        acc_ref[...] += partial_product()

    @pl.when(kk == last_k)
    def _():
        row_buf[j] = x_ref[...] + g_ref[...] * (acc_ref[...] + partial_product())

    @pl.when((kk == last_k) & (j == n_j - 1))
    def _():
        def body(grp, carry):
            @pl.when((grp >= n_slots) | (tile > 0))
            def _():
                for jj in range(n_j):
                    stage_copy(grp, jj).wait()

            base = pl.multiple_of(grp * group, group)
            for sub in range(group // rows_chunk):
                rows = pl.ds(base + sub * rows_chunk, rows_chunk)
                parts = [row_buf[jj, rows, :] for jj in range(n_j)]
                ssq = sum(jnp.sum(p * p, axis=-1, keepdims=True) for p in parts)
                inv = lax.rsqrt(ssq / (n_j * tn) + NORM_EPS)
                for jj in range(n_j):
                    stage[grp % n_slots, jj, sub * rows_chunk:(sub + 1) * rows_chunk, :] = (
                        parts[jj] * inv * fw_ref[:, jj * tn:(jj + 1) * tn])
            for jj in range(n_j):
                stage_copy(grp, jj).start(priority=jj % 2)
            return carry
        lax.fori_loop(0, n_groups, body, 0)

        @pl.when(tile == last_tile)
        def _():
            for grp in range(n_groups - n_slots, n_groups):
                for jj in range(n_j):
                    stage_copy(grp, jj).wait()


def _ff2_norm(a, w, x, gate, final_w, *, tm=1024, tn=1024, tk=2048):
    b, s, k = a.shape
    n = w.shape[1]
    tm, tn, tk = _tile(s, tm), _tile(n, tn), _tile(k, tk)
    n_j = n // tn
    group = NORM_UNROLL * NORM_ROWS
    n_groups = tm // group
    assert k // tk >= 2
    assert tm % group == 0 and n_groups % FF2_STAGE_SLOTS == 0
    return pl.pallas_call(
        functools.partial(_ff2_norm_kernel, rows_chunk=NORM_ROWS),
        grid=(b, s // tm, n // tn, k // tk),
        in_specs=[
            pl.BlockSpec((None, tm, tk), lambda bb, i, j, kk: (bb, i, kk)),
            pl.BlockSpec((tk, tn), lambda bb, i, j, kk: (kk, j)),
            pl.BlockSpec((None, tm, tn), lambda bb, i, j, kk: (bb, i, j)),
            pl.BlockSpec((None, 1, tn), lambda bb, i, j, kk: (bb, 0, j)),
            pl.BlockSpec((1, n), lambda bb, i, j, kk: (0, 0)),
        ],
        out_specs=pl.BlockSpec(memory_space=pl.ANY),
        out_shape=jax.ShapeDtypeStruct((b, s, n), F32),
        scratch_shapes=[pltpu.VMEM((tm, tn), F32), pltpu.VMEM((n_j, tm, tn), F32),
                        pltpu.VMEM((FF2_STAGE_SLOTS, n_j, group, tn), F32),
                        pltpu.SemaphoreType.DMA((FF2_STAGE_SLOTS, n_j))],
        compiler_params=_params(("arbitrary", "arbitrary", "arbitrary", "arbitrary"), 60),
        name="ff2_residual_norm",
    )(a, w, x, gate, final_w.reshape(1, n))


def kernel(x, c, ctx, c_ctx, w_ada, b_ada, norm1_w, w_in, rpb, sgu_norm_w, sgu_w, sgu_b,
           grp_norm_na, grp_norm_sgu, w_out, norm2_w, w_ff1, w_ff2, final_norm_w):
    b, s, d = x.shape
    n_ctx = ctx.shape[1]
    d_na = NA_HEADS * HEAD_DIM
    d_sgu = d - d_na
    depth = w_ada.shape[0]
    assert depth == 1

    n_rows = -(-(b + 1) // 8) * 8
    cvecs = jnp.zeros((n_rows, d), F32).at[:b].set(c).at[b].set(c_ctx)
    mod = _adaln(cvecs, w_ada[0], b_ada[0]).reshape(n_rows, N_MOD, d)
    lat = lambda i: mod[:b, i].reshape(b, 1, d)
    cx = lambda i: mod[b, i].reshape(1, 1, d)

    w_in_b = w_in[0].astype(BF16)
    qkv_scale = jnp.concatenate([jnp.full((d_na,), HEAD_DIM ** -0.5 * LOG2E, F32),
                                 jnp.ones((2 * d_na + 2 * d_sgu,), F32)])
    qkv, gated = _norm_mod_matmul(x, norm1_w[0], lat(1), lat(0), w_in_b, epilogue="heads_or_gelu",
                                  col_scale=qkv_scale, gelu_cols=2 * d_sgu, vmem_mib=60)
    ctx_kv = _norm_mod_matmul(ctx.reshape(1, b * n_ctx, d), norm1_w[0], cx(1), cx(0), w_in_b,
                              n_lo=d_na, n_hi=3 * d_na).reshape(b, n_ctx, 2 * d_na)
    o_na, w_out_b, w_ff1_b, w_ff2_b = _attention(qkv, ctx_kv, rpb[0], [w_out[0], w_ff1[0], w_ff2[0]])
    x1 = _mix_out(gated, o_na, x, lat(2), w_out_b, sgu_w[0], sgu_b[0], sgu_norm_w[0],
                  grp_norm_na[0], grp_norm_sgu[0])
    hidden = _ff1(x1, norm2_w[0], lat(4), lat(3), w_ff1_b)
    return _ff2_norm(hidden, w_ff2_b, x1, lat(5), final_norm_w)
```
